```python
import jax, jax.numpy as jnp
from jax import lax
import numpy as np

D_MODEL = 1024
BATCH = 2
SEQ = 8192
DEPTH = 4

N_MIXERS = 3
N_A = (DEPTH + 2) // 3
N_B = (DEPTH + 1) // 3
N_C = DEPTH // 3
SC_WIDTH = 3
ATTN_HEADS = 16
ATTN_KV_HEADS = 4
HEAD_DIM = 64
WINDOW = 128
BLOCK = 128
CONF_WIDTH = 31
MEM_LEN = 256
XATTN_HEADS = 4
XATTN_HEAD_DIM = D_MODEL // XATTN_HEADS
D_FF = 2816
FFN_CONV_WIDTH = 3
EPS = 1e-6

kernel_name = "hybrid_shortconv_swa_conformer_trunk"


def rmsnorm(x, g):
    x32 = x.astype(jnp.float32)
    y = x32 * lax.rsqrt(jnp.mean(x32 * x32, axis=-1, keepdims=True) + EPS)
    return y.astype(x.dtype) * g


def layernorm(x, g, b):
    x32 = x.astype(jnp.float32)
    mu = jnp.mean(x32, axis=-1, keepdims=True)
    xc = x32 - mu
    y = xc * lax.rsqrt(jnp.mean(xc * xc, axis=-1, keepdims=True) + EPS)
    return y.astype(x.dtype) * g + b


def causal_dwconv(x, w):
    K, C = w.shape
    return lax.conv_general_dilated(
        x, w[:, None, :], window_strides=(1,), padding=[(K - 1, 0)],
        dimension_numbers=('NWC', 'WIO', 'NWC'), feature_group_count=C)


def alibi_slopes(n_heads):
    return 2.0 ** (-8.0 * jnp.arange(1, n_heads + 1, dtype=jnp.float32) / n_heads)


def short_conv_mixer(h, w_in, conv_w, w_out):
    b_gate, c_gate, v = jnp.split(h @ w_in, 3, axis=-1)
    return (b_gate * causal_dwconv(c_gate * v, conv_w)) @ w_out


def swa_sink_attention(h, w_qkv, b_qkv, sinks, w_o, b_o):
    bsz, s_len, _ = h.shape
    G = ATTN_HEADS // ATTN_KV_HEADS
    NB = s_len // BLOCK
    qkv = h @ w_qkv + b_qkv
    q, k, v = jnp.split(qkv, [ATTN_HEADS * HEAD_DIM, (ATTN_HEADS + ATTN_KV_HEADS) * HEAD_DIM], axis=-1)
    q = q.reshape(bsz, NB, BLOCK, ATTN_KV_HEADS, G, HEAD_DIM)

    def band(t):
        t = jnp.pad(t, ((0, 0), (BLOCK, 0), (0, 0))).reshape(bsz, NB + 1, BLOCK, ATTN_KV_HEADS, HEAD_DIM)
        return jnp.concatenate([t[:, :-1], t[:, 1:]], axis=2)

    kb, vb = band(k), band(v)
    s = jnp.einsum('bnqkgd,bnskd->bnkgqs', q, kb, preferred_element_type=jnp.float32) * (HEAD_DIM ** -0.5)
    dist = BLOCK + jnp.arange(BLOCK)[:, None] - jnp.arange(2 * BLOCK)[None, :]
    key_pos = jnp.arange(NB)[:, None] * BLOCK - BLOCK + jnp.arange(2 * BLOCK)[None, :]
    valid = ((dist >= 0) & (dist < WINDOW))[None] & (key_pos >= 0)[:, None, :]
    slopes = alibi_slopes(ATTN_HEADS).reshape(ATTN_KV_HEADS, G)
    s = s - slopes[:, :, None, None] * dist.astype(jnp.float32)
    s = jnp.where(valid[None, :, None, None], s, -jnp.inf)
    sink = sinks.astype(jnp.float32).reshape(ATTN_KV_HEADS, G)[None, None, :, :, None]
    m = jnp.maximum(s.max(axis=-1), sink)
    p = jnp.exp(s - m[..., None])
    p = p / (p.sum(axis=-1) + jnp.exp(sink - m))[..., None]
    o = jnp.einsum('bnkgqs,bnskd->bnqkgd', p.astype(vb.dtype), vb)
    return o.reshape(bsz, s_len, ATTN_HEADS * HEAD_DIM) @ w_o + b_o


def conformer_conv_module(h, w_pw1, b_pw1, w_dw, b_dw, ln_g, ln_b, w_pw2, b_pw2):
    a, gate = jnp.split(h @ w_pw1 + b_pw1, 2, axis=-1)
    u = a * jax.nn.sigmoid(gate)
    u = causal_dwconv(u, w_dw) + b_dw
    u = layernorm(u, ln_g, ln_b)
    return jax.nn.silu(u) @ w_pw2 + b_pw2


def memory_cross_attention(h, mem, g_mem, w_q, w_kv, w_o):
    bsz, s_len, _ = h.shape
    q = (h @ w_q).reshape(bsz, s_len, XATTN_HEADS, XATTN_HEAD_DIM)
    k, v = jnp.split(rmsnorm(mem, g_mem) @ w_kv, 2, axis=-1)
    k = k.reshape(bsz, -1, XATTN_HEADS, XATTN_HEAD_DIM)
    v = v.reshape(bsz, -1, XATTN_HEADS, XATTN_HEAD_DIM)
    s = jnp.einsum('bshd,bmhd->bhsm', q, k, preferred_element_type=jnp.float32) * (XATTN_HEAD_DIM ** -0.5)
    p = jax.nn.softmax(s, axis=-1)
    o = jnp.einsum('bhsm,bmhd->bshd', p.astype(v.dtype), v).reshape(bsz, s_len, D_MODEL)
    return o @ w_o


def conv_gated_mlp(h, w_up, conv_w, w_down):
    gate, up = jnp.split(h @ w_up, 2, axis=-1)
    return (jax.nn.silu(causal_dwconv(gate, conv_w)) * up) @ w_down


def setup_inputs(seed: int = 0) -> dict:
    key = jax.random.key(seed)
    ks = iter(jax.random.split(key, 40))
    D = D_MODEL
    QKV = (ATTN_HEADS + 2 * ATTN_KV_HEADS) * HEAD_DIM

    def nrm(shape, scale):
        return jax.random.normal(next(ks), shape, jnp.float32) * scale

    def gain(shape):
        return 1.0 + nrm(shape, 0.02)

    return {
        "x": nrm((BATCH, SEQ, D), 1.0),
        "mem": nrm((BATCH, MEM_LEN, D), 1.0),
        "norm_mix": gain((DEPTH, D)),
        "norm_xattn": gain((DEPTH, D)),
        "norm_mem": gain((DEPTH, D)),
        "norm_ffn": gain((DEPTH, D)),
        "norm_final": gain((D,)),
        "a_w_in": nrm((N_A, D, 3 * D), D ** -0.5),
        "a_conv": nrm((N_A, SC_WIDTH, D), SC_WIDTH ** -0.5),
        "a_w_out": nrm((N_A, D, D), D ** -0.5),
        "b_w_qkv": nrm((N_B, D, QKV), D ** -0.5),
        "b_b_qkv": nrm((N_B, QKV), 0.02),
        "b_sinks": nrm((N_B, ATTN_HEADS), 0.5),
        "b_w_o": nrm((N_B, ATTN_HEADS * HEAD_DIM, D), (ATTN_HEADS * HEAD_DIM) ** -0.5),
        "b_b_o": nrm((N_B, D), 0.02),
        "c_w_pw1": nrm((N_C, D, 2 * D), D ** -0.5),
        "c_b_pw1": nrm((N_C, 2 * D), 0.02),
        "c_w_dw": nrm((N_C, CONF_WIDTH, D), CONF_WIDTH ** -0.5),
        "c_b_dw": nrm((N_C, D), 0.02),
        "c_ln_g": gain((N_C, D)),
        "c_ln_b": nrm((N_C, D), 0.02),
        "c_w_pw2": nrm((N_C, D, D), D ** -0.5),
        "c_b_pw2": nrm((N_C, D), 0.02),
        "x_w_q": nrm((DEPTH, D, D), D ** -0.5),
        "x_w_kv": nrm((DEPTH, D, 2 * D), D ** -0.5),
        "x_w_o": nrm((DEPTH, D, D), D ** -0.5),
        "f_w_up": nrm((DEPTH, D, 2 * D_FF), D ** -0.5),
        "f_conv": nrm((DEPTH, FFN_CONV_WIDTH, D_FF), FFN_CONV_WIDTH ** -0.5),
        "f_w_down": nrm((DEPTH, D_FF, D), D_FF ** -0.5),
    }


def reference(x, mem, norm_mix, norm_xattn, norm_mem, norm_ffn, norm_final,
              a_w_in, a_conv, a_w_out,
              b_w_qkv, b_b_qkv, b_sinks, b_w_o, b_b_o,
              c_w_pw1, c_b_pw1, c_w_dw, c_b_dw, c_ln_g, c_ln_b, c_w_pw2, c_b_pw2,
              x_w_q, x_w_kv, x_w_o,
              f_w_up, f_conv, f_w_down):
    for i in range(DEPTH):
        kind, j = i % N_MIXERS, i // N_MIXERS
        h = rmsnorm(x, norm_mix[i])
        if kind == 0:
            y = short_conv_mixer(h, a_w_in[j], a_conv[j], a_w_out[j])
        elif kind == 1:
            y = swa_sink_attention(h, b_w_qkv[j], b_b_qkv[j], b_sinks[j], b_w_o[j], b_b_o[j])
        else:
            y = conformer_conv_module(h, c_w_pw1[j], c_b_pw1[j], c_w_dw[j], c_b_dw[j],
                                      c_ln_g[j], c_ln_b[j], c_w_pw2[j], c_b_pw2[j])
        x = x + y
        x = x + memory_cross_attention(rmsnorm(x, norm_xattn[i]), mem, norm_mem[i],
                                       x_w_q[i], x_w_kv[i], x_w_o[i])
        x = x + conv_gated_mlp(rmsnorm(x, norm_ffn[i]), f_w_up[i], f_conv[i], f_w_down[i])
    return rmsnorm(x, norm_final)
```

```python
import functools

import jax
import jax.numpy as jnp
from jax import lax
from jax.experimental import pallas as pl
from jax.experimental.pallas import tpu as pltpu

EPS = 1e-6
F32 = jnp.float32
BF16 = jnp.bfloat16

D_MODEL = 1024
SC_WIDTH = 3
ATTN_HEADS = 16
ATTN_KV_HEADS = 4
HEAD_DIM = 64
WINDOW = 128
BLOCK = 128
CONF_WIDTH = 31
XATTN_HEADS = 4
XATTN_HEAD_DIM = D_MODEL // XATTN_HEADS
D_FF = 2816
FFN_CONV_WIDTH = 3

SUBLANES = 8
LANES = 128
MXU_DIM = 256
VMEM_LIMIT_BYTES = 56 * 1024 * 1024

TM = 512
CHUNK = MXU_DIM
CONF_CARRY = 32


def _rms(x, g):
    ms = jnp.mean(x * x, axis=-1, keepdims=True)
    return x * lax.rsqrt(ms + EPS) * g


def _dot(a, b):
    return jnp.dot(a, b, preferred_element_type=F32)


def _dot_nt(a, b):
    return lax.dot_general(a, b, (((1,), (1,)), ((), ())), preferred_element_type=F32)


def _sigmoid(x):
    return 1.0 / (1.0 + jnp.exp(-x))


def _shifted(prev, cur, shift):
    n_prev = prev.shape[0]
    ext = jnp.concatenate([prev, cur], axis=0)
    return pltpu.roll(ext, shift, 0)[n_prev:]


def _resident(shape):
    nd = len(shape)
    return pl.BlockSpec(shape, lambda *_: (0,) * nd, pipeline_mode=pl.Buffered(1))


def _params():
    return pltpu.CompilerParams(dimension_semantics=("arbitrary",),
                                vmem_limit_bytes=VMEM_LIMIT_BYTES)


def _sconv_kernel(x_ref, g_ref, win_ref, cw_ref, wout_ref, o_ref, carry_ref, z_ref,
                  *, tiles_per_seq):
    tm = x_ref.shape[0]
    first = (pl.program_id(0) % tiles_per_seq) == 0
    x = x_ref[...]
    xn = _rms(x, g_ref[...]).astype(BF16)
    cw = cw_ref[...]
    for j in range(D_MODEL // CHUNK):
        lo, hi = j * CHUNK, (j + 1) * CHUNK
        b = _dot(xn, win_ref[:, lo:hi])
        c = _dot(xn, win_ref[:, D_MODEL + lo:D_MODEL + hi])
        v = _dot(xn, win_ref[:, 2 * D_MODEL + lo:2 * D_MODEL + hi])
        u = c * v
        prev = jnp.where(first, 0.0, carry_ref[:, lo:hi])
        carry_ref[:, lo:hi] = u[tm - SUBLANES:, :]
        conv = (cw[2:3, lo:hi] * u
                + cw[1:2, lo:hi] * _shifted(prev, u, 1)
                + cw[0:1, lo:hi] * _shifted(prev, u, 2))
        z_ref[:, lo:hi] = (b * conv).astype(BF16)
    o_ref[...] = x + _dot(z_ref[...], wout_ref[...])


def _sconv_mixer(x, g, w_in, conv_w, w_out, *, seq):
    n, d = x.shape
    kern = functools.partial(_sconv_kernel, tiles_per_seq=seq // TM)
    return pl.pallas_call(
        kern,
        out_shape=jax.ShapeDtypeStruct((n, d), F32),
        grid=(n // TM,),
        in_specs=[
            pl.BlockSpec((TM, d), lambda i: (i, 0)),
            _resident((1, d)),
            _resident(w_in.shape),
            _resident(conv_w.shape),
            _resident(w_out.shape),
        ],
        out_specs=pl.BlockSpec((TM, d), lambda i: (i, 0)),
        scratch_shapes=[pltpu.VMEM((SUBLANES, d), F32),
                        pltpu.VMEM((TM, d), BF16)],
        compiler_params=_params(),
        name="sconv_mixer",
    )(x, g, w_in, conv_w, w_out)


def _swa_kernel(sinks_ref, x_ref, g_ref, wqkv_ref, bqkv_ref, wo_ref, bo_ref, o_ref,
                q_s, k2_s, v2_s, oh_s, *, tiles_per_seq):
    tm = x_ref.shape[0]
    n_blocks = tm // BLOCK
    group = ATTN_HEADS // ATTN_KV_HEADS
    q_dim = ATTN_HEADS * HEAD_DIM
    kv_dim = ATTN_KV_HEADS * HEAD_DIM
    first = (pl.program_id(0) % tiles_per_seq) == 0

    @pl.when(first)
    def _():
        k2_s[:, :, 0:BLOCK, :] = jnp.zeros((ATTN_KV_HEADS, 2, BLOCK, LANES), BF16)
        v2_s[:, :, 0:BLOCK, :] = jnp.zeros((ATTN_KV_HEADS, 2, BLOCK, LANES), BF16)

    x = x_ref[...]
    xn = _rms(x, g_ref[...]).astype(BF16)
    for j in range(q_dim // CHUNK):
        lo, hi = j * CHUNK, (j + 1) * CHUNK
        q_s[:, lo:hi] = (_dot(xn, wqkv_ref[:, lo:hi]) + bqkv_ref[:, lo:hi]).astype(BF16)

    lane = lax.broadcasted_iota(jnp.int32, (tm, LANES), 1)
    for (dst, base) in ((k2_s, q_dim), (v2_s, q_dim + kv_dim)):
        for gi in range(kv_dim // LANES):
            lo = base + gi * LANES
            kg = _dot(xn, wqkv_ref[:, lo:lo + LANES]) + bqkv_ref[:, lo:lo + LANES]
            low = jnp.where(lane < HEAD_DIM, kg, 0.0)
            high = jnp.where(lane >= HEAD_DIM, kg, 0.0)
            low_hi = pltpu.roll(low, HEAD_DIM, 1)
            high_lo = pltpu.roll(high, HEAD_DIM, 1)
            dst[2 * gi, 0, BLOCK:, :] = low.astype(BF16)
            dst[2 * gi, 1, BLOCK:, :] = low_hi.astype(BF16)
            dst[2 * gi + 1, 0, BLOCK:, :] = high_lo.astype(BF16)
            dst[2 * gi + 1, 1, BLOCK:, :] = high.astype(BF16)

    qi = lax.broadcasted_iota(jnp.int32, (BLOCK, 2 * BLOCK), 0)
    kj = lax.broadcasted_iota(jnp.int32, (BLOCK, 2 * BLOCK), 1)
    dist = BLOCK + qi - kj
    valid = (dist >= 0) & (dist < WINDOW)
    distf = dist.astype(F32)
    neg_inf = jnp.float32(-jnp.inf)
    scale = HEAD_DIM ** -0.5

    def block_body(n, carry):
        row = pl.multiple_of(n * BLOCK, BLOCK)
        pad_pen = jnp.where(jnp.logical_and(first, n == 0), neg_inf, jnp.float32(0.0))
        pen = jnp.where(valid, 0.0, neg_inf) + jnp.where(kj < BLOCK, pad_pen, 0.0)
        for p in range(ATTN_HEADS // 2):
            kh = (2 * p) // group
            qp = q_s[pl.ds(row, BLOCK), p * LANES:(p + 1) * LANES]
            o_pair = None
            for half in range(2):
                h = 2 * p + half
                slope = 2.0 ** (-8.0 * (h + 1) / ATTN_HEADS)
                kb = k2_s[kh, half, pl.ds(row, 2 * BLOCK), :]
                vb = v2_s[kh, half, pl.ds(row, 2 * BLOCK), :]
                s = _dot_nt(qp, kb) * scale - slope * distf + pen
                sink = sinks_ref[h]
                m = jnp.maximum(jnp.max(s, axis=-1, keepdims=True), sink)
                e = jnp.exp(s - m)
                denom = jnp.sum(e, axis=-1, keepdims=True) + jnp.exp(sink - m)
                o_h = _dot(e.astype(BF16), vb) / denom
                o_pair = o_h if o_pair is None else o_pair + o_h
            oh_s[pl.ds(row, BLOCK), p * LANES:(p + 1) * LANES] = o_pair.astype(BF16)
        return carry

    lax.fori_loop(0, n_blocks, block_body, 0)

    k2_s[:, :, 0:BLOCK, :] = k2_s[:, :, tm:tm + BLOCK, :]
    v2_s[:, :, 0:BLOCK, :] = v2_s[:, :, tm:tm + BLOCK, :]

    o_ref[...] = x + _dot(oh_s[...], wo_ref[...]) + bo_ref[...]


def _swa_mixer(x, g, w_qkv, b_qkv, sinks, w_o, b_o, *, seq):
    n, d = x.shape
    kern = functools.partial(_swa_kernel, tiles_per_seq=seq // TM)
    q_dim = ATTN_HEADS * HEAD_DIM
    return pl.pallas_call(
        kern,
        out_shape=jax.ShapeDtypeStruct((n, d), F32),
        grid=(n // TM,),
        in_specs=[
            pl.BlockSpec(memory_space=pltpu.SMEM),
            pl.BlockSpec((TM, d), lambda i: (i, 0)),
            _resident((1, d)),
            _resident(w_qkv.shape),
            _resident(b_qkv.shape),
            _resident(w_o.shape),
            _resident(b_o.shape),
        ],
        out_specs=pl.BlockSpec((TM, d), lambda i: (i, 0)),
        scratch_shapes=[
            pltpu.VMEM((TM, q_dim), BF16),
            pltpu.VMEM((ATTN_KV_HEADS, 2, BLOCK + TM, LANES), BF16),
            pltpu.VMEM((ATTN_KV_HEADS, 2, BLOCK + TM, LANES), BF16),
            pltpu.VMEM((TM, q_dim), BF16),
        ],
        compiler_params=_params(),
        name="swa_mixer",
    )(sinks, x, g, w_qkv, b_qkv, w_o, b_o)


def _conformer_kernel(x_ref, g_ref, w1_ref, b1_ref, wdw_ref, bdw_ref, lng_ref, lnb_ref,
                      w2_ref, b2_ref, o_ref, carry_ref, u_ref, *, tiles_per_seq):
    tm = x_ref.shape[0]
    first = (pl.program_id(0) % tiles_per_seq) == 0
    x = x_ref[...]
    xn = _rms(x, g_ref[...]).astype(BF16)
    for j in range(D_MODEL // CHUNK):
        lo, hi = j * CHUNK, (j + 1) * CHUNK
        a = _dot(xn, w1_ref[:, lo:hi]) + b1_ref[:, lo:hi]
        gate = _dot(xn, w1_ref[:, D_MODEL + lo:D_MODEL + hi]) + b1_ref[:, D_MODEL + lo:D_MODEL + hi]
        u = a * _sigmoid(gate)
        prev = jnp.where(first, 0.0, carry_ref[:, lo:hi])
        carry_ref[:, lo:hi] = u[tm - CONF_CARRY:, :]
        ext = jnp.concatenate([prev, u], axis=0)
        acc = jnp.zeros((tm, CHUNK), F32) + bdw_ref[:, lo:hi]
        for r in range(SUBLANES):
            ext_r = ext if r == 0 else pltpu.roll(ext, r, 0)
            for a8 in range(CONF_CARRY // SUBLANES):
                delay = a8 * SUBLANES + r
                if delay >= CONF_WIDTH:
                    continue
                tap = CONF_WIDTH - 1 - delay
                start = CONF_CARRY - a8 * SUBLANES
                acc = acc + wdw_ref[tap:tap + 1, lo:hi] * ext_r[start:start + tm, :]
        u_ref[:, lo:hi] = acc
    u = u_ref[...]
    mu = jnp.mean(u, axis=-1, keepdims=True)
    uc = u - mu
    y = uc * lax.rsqrt(jnp.mean(uc * uc, axis=-1, keepdims=True) + EPS)
    y = y * lng_ref[...] + lnb_ref[...]
    y = (y * _sigmoid(y)).astype(BF16)
    o_ref[...] = x + _dot(y, w2_ref[...]) + b2_ref[...]


def _conformer_mixer(x, g, w1, b1, wdw, bdw, lng, lnb, w2, b2, *, seq):
    n, d = x.shape
    kern = functools.partial(_conformer_kernel, tiles_per_seq=seq // TM)
    return pl.pallas_call(
        kern,
        out_shape=jax.ShapeDtypeStruct((n, d), F32),
        grid=(n // TM,),
        in_specs=[pl.BlockSpec((TM, d), lambda i: (i, 0))]
        + [_resident(a.shape) for a in (g, w1, b1, wdw, bdw, lng, lnb, w2, b2)],
        out_specs=pl.BlockSpec((TM, d), lambda i: (i, 0)),
        scratch_shapes=[pltpu.VMEM((CONF_CARRY, d), F32),
                        pltpu.VMEM((TM, d), F32)],
        compiler_params=_params(),
        name="conformer_mixer",
    )(x, g, w1, b1, wdw, bdw, lng, lnb, w2, b2)


def _memkv_kernel(mem_ref, g_ref, wkv_ref, k_ref, v_ref):
    d = mem_ref.shape[-1]
    mn = _rms(mem_ref[...], g_ref[...]).astype(BF16)
    k_ref[...] = _dot(mn, wkv_ref[:, :d]).astype(BF16)
    v_ref[...] = _dot(mn, wkv_ref[:, d:]).astype(BF16)


def _memory_kv(mem, g_mem, w_kv):
    bsz, m, d = mem.shape
    depth = w_kv.shape[0]
    out = jax.ShapeDtypeStruct((depth, bsz, m, d), BF16)
    return pl.pallas_call(
        _memkv_kernel,
        out_shape=(out, out),
        grid=(depth, bsz),
        in_specs=[
            pl.BlockSpec((None, m, d), lambda l, b: (b, 0, 0)),
            pl.BlockSpec((None, 1, d), lambda l, b: (l, 0, 0)),
            pl.BlockSpec((None, d, 2 * d), lambda l, b: (l, 0, 0)),
        ],
        out_specs=(pl.BlockSpec((None, None, m, d), lambda l, b: (l, b, 0, 0)),
                   pl.BlockSpec((None, None, m, d), lambda l, b: (l, b, 0, 0))),
        compiler_params=pltpu.CompilerParams(
            dimension_semantics=("arbitrary", "arbitrary"),
            vmem_limit_bytes=VMEM_LIMIT_BYTES),
        name="memory_kv",
    )(mem, g_mem, w_kv)


def _xattn_kernel(x_ref, g_ref, wq_ref, k_ref, v_ref, wo_ref, o_ref, oh_s):
    hd = XATTN_HEAD_DIM
    x = x_ref[...]
    xn = _rms(x, g_ref[...]).astype(BF16)
    scale = hd ** -0.5
    for h in range(XATTN_HEADS):
        lo, hi = h * hd, (h + 1) * hd
        q = _dot(xn, wq_ref[:, lo:hi]).astype(BF16)
        s = _dot_nt(q, k_ref[:, lo:hi]) * scale
        m = jnp.max(s, axis=-1, keepdims=True)
        e = jnp.exp(s - m)
        denom = jnp.sum(e, axis=-1, keepdims=True)
        o = _dot(e.astype(BF16), v_ref[:, lo:hi]) / denom
        oh_s[:, lo:hi] = o.astype(BF16)
    o_ref[...] = x + _dot(oh_s[...], wo_ref[...])


def _memory_xattn(x, g, w_q, k, v, w_o, *, seq):
    n, d = x.shape
    m = k.shape[1]
    tiles_per_seq = seq // TM
    return pl.pallas_call(
        _xattn_kernel,
        out_shape=jax.ShapeDtypeStruct((n, d), F32),
        grid=(n // TM,),
        in_specs=[
            pl.BlockSpec((TM, d), lambda i: (i, 0)),
            _resident((1, d)),
            _resident(w_q.shape),
            pl.BlockSpec((None, m, d), lambda i: (i // tiles_per_seq, 0, 0)),
            pl.BlockSpec((None, m, d), lambda i: (i // tiles_per_seq, 0, 0)),
            _resident(w_o.shape),
        ],
        out_specs=pl.BlockSpec((TM, d), lambda i: (i, 0)),
        scratch_shapes=[pltpu.VMEM((TM, d), BF16)],
        compiler_params=_params(),
        name="memory_xattn",
    )(x, g, w_q, k, v, w_o)


def _ffn_kernel(x_ref, g_ref, wg_ref, wu_ref, cw_ref, wd_ref, gf_ref, o_ref,
                xn_s, carry_s, *, tiles_per_seq, final_norm):
    tm = x_ref.shape[0]
    n_chunks = wg_ref.shape[0]
    first = (pl.program_id(0) % tiles_per_seq) == 0
    x = x_ref[...]
    xn_s[...] = _rms(x, g_ref[...]).astype(BF16)
    o_ref[...] = x

    def chunk_body(j, carry):
        xn = xn_s[...]
        gate = _dot(xn, wg_ref[j])
        up = _dot(xn, wu_ref[j])
        prev = jnp.where(first, 0.0, carry_s[j])
        carry_s[j] = gate[tm - SUBLANES:, :]
        cw = cw_ref[j]
        cv = (cw[2:3] * gate
              + cw[1:2] * _shifted(prev, gate, 1)
              + cw[0:1] * _shifted(prev, gate, 2))
        act = (cv * _sigmoid(cv) * up).astype(BF16)
        o_ref[...] += _dot(act, wd_ref[j])
        return carry

    lax.fori_loop(0, n_chunks, chunk_body, 0)
    if final_norm:
        o_ref[...] = _rms(o_ref[...], gf_ref[...])


def _conv_gated_mlp(x, g, wg, wu, cw, wd, g_final, *, seq, final_norm):
    n, d = x.shape
    n_chunks, _, fc = wg.shape
    kern = functools.partial(_ffn_kernel, tiles_per_seq=seq // TM, final_norm=final_norm)
    return pl.pallas_call(
        kern,
        out_shape=jax.ShapeDtypeStruct((n, d), F32),
        grid=(n // TM,),
        in_specs=[pl.BlockSpec((TM, d), lambda i: (i, 0))]
        + [_resident(a.shape) for a in (g, wg, wu, cw, wd, g_final)],
        out_specs=pl.BlockSpec((TM, d), lambda i: (i, 0)),
        scratch_shapes=[pltpu.VMEM((TM, d), BF16),
                        pltpu.VMEM((n_chunks, SUBLANES, fc), F32)],
        compiler_params=_params(),
        name="conv_gated_mlp",
    )(x, g, wg, wu, cw, wd, g_final)


def _chunk_cols(w, n_chunks):
    k, cols = w.shape
    return w.reshape(k, n_chunks, cols // n_chunks).transpose(1, 0, 2)


def kernel(x, mem, norm_mix, norm_xattn, norm_mem, norm_ffn, norm_final, a_w_in, a_conv, a_w_out, b_w_qkv, b_b_qkv, b_sinks, b_w_o, b_b_o, c_w_pw1, c_b_pw1, c_w_dw, c_b_dw, c_ln_g, c_ln_b, c_w_pw2, c_b_pw2, x_w_q, x_w_kv, x_w_o, f_w_up, f_conv, f_w_down):
    bsz, seq, d = x.shape
    depth = norm_mix.shape[0]
    assert d == D_MODEL and seq % TM == 0 and TM % BLOCK == 0
    n_chunks = D_FF // CHUNK

    h = x.reshape(bsz * seq, d)
    k_all, v_all = _memory_kv(mem, norm_mem.reshape(depth, 1, d), x_w_kv.astype(BF16))

    for i in range(depth):
        kind, j = i % 3, i // 3
        g_mix = norm_mix[i].reshape(1, d)
        if kind == 0:
            h = _sconv_mixer(h, g_mix, a_w_in[j].astype(BF16), a_conv[j],
                             a_w_out[j].astype(BF16), seq=seq)
        elif kind == 1:
            h = _swa_mixer(h, g_mix, b_w_qkv[j].astype(BF16), b_b_qkv[j].reshape(1, -1),
                           b_sinks[j], b_w_o[j].astype(BF16), b_b_o[j].reshape(1, d), seq=seq)
        else:
            h = _conformer_mixer(h, g_mix, c_w_pw1[j].astype(BF16), c_b_pw1[j].reshape(1, -1),
                                 c_w_dw[j], c_b_dw[j].reshape(1, d), c_ln_g[j].reshape(1, d),
                                 c_ln_b[j].reshape(1, d), c_w_pw2[j].astype(BF16),
                                 c_b_pw2[j].reshape(1, d), seq=seq)
        h = _memory_xattn(h, norm_xattn[i].reshape(1, d), x_w_q[i].astype(BF16),
                          k_all[i], v_all[i], x_w_o[i].astype(BF16), seq=seq)
        w_up = f_w_up[i].astype(BF16)
        wg = _chunk_cols(w_up[:, :D_FF], n_chunks)
        wu = _chunk_cols(w_up[:, D_FF:], n_chunks)
        cw = f_conv[i].reshape(FFN_CONV_WIDTH, n_chunks, CHUNK).transpose(1, 0, 2)
        wd = f_w_down[i].astype(BF16).reshape(n_chunks, CHUNK, d)
        h = _conv_gated_mlp(h, norm_ffn[i].reshape(1, d), wg, wu, cw, wd,
                            norm_final.reshape(1, d), seq=seq,
                            final_norm=(i == depth - 1))
    return h.reshape(bsz, seq, d)
```

```python
import functools

import jax
import jax.numpy as jnp
from jax import lax
from jax.experimental import pallas as pl
from jax.experimental.pallas import tpu as pltpu

EPS = 1e-6
F32 = jnp.float32
BF16 = jnp.bfloat16

D_MODEL = 1024
SC_WIDTH = 3
ATTN_HEADS = 16
ATTN_KV_HEADS = 4
HEAD_DIM = 64
WINDOW = 128
BLOCK = 128
CONF_WIDTH = 31
XATTN_HEADS = 4
XATTN_HEAD_DIM = D_MODEL // XATTN_HEADS
D_FF = 2816
FFN_CONV_WIDTH = 3

SUBLANES = 8
LANES = 128
MXU_DIM = 256
VMEM_LIMIT_BYTES = 56 * 1024 * 1024

TM = 512
CHUNK = MXU_DIM
CONF_CARRY = 32


def _rms(x, g):
    ms = jnp.mean(x * x, axis=-1, keepdims=True)
    return x * lax.rsqrt(ms + EPS) * g


def _dot(a, b):
    return jnp.dot(a, b, preferred_element_type=F32)


def _dot_nt(a, b):
    return lax.dot_general(a, b, (((1,), (1,)), ((), ())), preferred_element_type=F32)


def _sigmoid(x):
    return 1.0 / (1.0 + jnp.exp(-x))


def _shifted(prev, cur, shift):
    n_prev = prev.shape[0]
    ext = jnp.concatenate([prev, cur], axis=0)
    return pltpu.roll(ext, shift, 0)[n_prev:]


def _resident(shape):
    nd = len(shape)
    return pl.BlockSpec(shape, lambda *_: (0,) * nd, pipeline_mode=pl.Buffered(1))


def _params():
    return pltpu.CompilerParams(dimension_semantics=("arbitrary",),
                                vmem_limit_bytes=VMEM_LIMIT_BYTES)


def _sconv_kernel(x_ref, g_ref, win_ref, cw_ref, wout_ref, o_ref, carry_ref, z_ref,
                  *, tiles_per_seq):
    tm = x_ref.shape[0]
    first = (pl.program_id(0) % tiles_per_seq) == 0
    x = x_ref[...]
    xn = _rms(x, g_ref[...]).astype(BF16)
    cw = cw_ref[...]
    for j in range(D_MODEL // CHUNK):
        lo, hi = j * CHUNK, (j + 1) * CHUNK
        b = _dot(xn, win_ref[:, lo:hi])
        c = _dot(xn, win_ref[:, D_MODEL + lo:D_MODEL + hi])
        v = _dot(xn, win_ref[:, 2 * D_MODEL + lo:2 * D_MODEL + hi])
        u = c * v
        prev = jnp.where(first, 0.0, carry_ref[:, lo:hi])
        carry_ref[:, lo:hi] = u[tm - SUBLANES:, :]
        conv = (cw[2:3, lo:hi] * u
                + cw[1:2, lo:hi] * _shifted(prev, u, 1)
                + cw[0:1, lo:hi] * _shifted(prev, u, 2))
        z_ref[:, lo:hi] = (b * conv).astype(BF16)
    o_ref[...] = x + _dot(z_ref[...], wout_ref[...])


def _sconv_mixer(x, g, w_in, conv_w, w_out, *, seq):
    n, d = x.shape
    kern = functools.partial(_sconv_kernel, tiles_per_seq=seq // TM)
    return pl.pallas_call(
        kern,
        out_shape=jax.ShapeDtypeStruct((n, d), F32),
        grid=(n // TM,),
        in_specs=[
            pl.BlockSpec((TM, d), lambda i: (i, 0)),
            _resident((1, d)),
            _resident(w_in.shape),
            _resident(conv_w.shape),
            _resident(w_out.shape),
        ],
        out_specs=pl.BlockSpec((TM, d), lambda i: (i, 0)),
        scratch_shapes=[pltpu.VMEM((SUBLANES, d), F32),
                        pltpu.VMEM((TM, d), BF16)],
        compiler_params=_params(),
        name="sconv_mixer",
    )(x, g, w_in, conv_w, w_out)


def _swa_kernel(sinks_ref, x_ref, g_ref, wqkv_ref, bqkv_ref, wo_ref, bo_ref, o_ref,
                q_s, k2_s, v2_s, oh_s, bias_s, *, tiles_per_seq):
    tm = x_ref.shape[0]
    n_blocks = tm // BLOCK
    group = ATTN_HEADS // ATTN_KV_HEADS
    q_dim = ATTN_HEADS * HEAD_DIM
    kv_dim = ATTN_KV_HEADS * HEAD_DIM
    first = (pl.program_id(0) % tiles_per_seq) == 0

    @pl.when(first)
    def _():
        k2_s[:, :, 0:BLOCK, :] = jnp.zeros((ATTN_KV_HEADS, 2, BLOCK, LANES), BF16)
        v2_s[:, :, 0:BLOCK, :] = jnp.zeros((ATTN_KV_HEADS, 2, BLOCK, LANES), BF16)

    @pl.when(pl.program_id(0) == 0)
    def _():
        qi = lax.broadcasted_iota(jnp.int32, (BLOCK, 2 * BLOCK), 0)
        kj = lax.broadcasted_iota(jnp.int32, (BLOCK, 2 * BLOCK), 1)
        dist = BLOCK + qi - kj
        neg_inf = jnp.float32(-jnp.inf)
        pen = jnp.where((dist >= 0) & (dist < WINDOW), 0.0, neg_inf)
        pen_start = jnp.where(kj < BLOCK, neg_inf, pen)
        distf = dist.astype(F32)
        for h in range(ATTN_HEADS):
            slope = 2.0 ** (-8.0 * (h + 1) / ATTN_HEADS)
            bias_s[0, h] = pen - slope * distf
            bias_s[1, h] = pen_start - slope * distf

    x = x_ref[...]
    xn = _rms(x, g_ref[...]).astype(BF16)
    scale = HEAD_DIM ** -0.5
    for j in range(q_dim // CHUNK):
        lo, hi = j * CHUNK, (j + 1) * CHUNK
        q_s[:, lo:hi] = ((_dot(xn, wqkv_ref[:, lo:hi]) + bqkv_ref[:, lo:hi]) * scale).astype(BF16)

    lane = lax.broadcasted_iota(jnp.int32, (tm, LANES), 1)
    for (dst, base) in ((k2_s, q_dim), (v2_s, q_dim + kv_dim)):
        for gi in range(kv_dim // LANES):
            lo = base + gi * LANES
            kg = _dot(xn, wqkv_ref[:, lo:lo + LANES]) + bqkv_ref[:, lo:lo + LANES]
            low = jnp.where(lane < HEAD_DIM, kg, 0.0)
            high = jnp.where(lane >= HEAD_DIM, kg, 0.0)
            low_hi = pltpu.roll(low, HEAD_DIM, 1)
            high_lo = pltpu.roll(high, HEAD_DIM, 1)
            dst[2 * gi, 0, BLOCK:, :] = low.astype(BF16)
            dst[2 * gi, 1, BLOCK:, :] = low_hi.astype(BF16)
            dst[2 * gi + 1, 0, BLOCK:, :] = high_lo.astype(BF16)
            dst[2 * gi + 1, 1, BLOCK:, :] = high.astype(BF16)

    def block_body(n, carry):
        row = pl.multiple_of(n * BLOCK, BLOCK)
        variant = jnp.where(jnp.logical_and(first, n == 0), 1, 0)
        for kh in range(ATTN_KV_HEADS):
            heads = range(kh * group, (kh + 1) * group)
            scores = []
            for h in heads:
                qp = q_s[pl.ds(row, BLOCK), (h // 2) * LANES:(h // 2 + 1) * LANES]
                kb = k2_s[kh, h % 2, pl.ds(row, 2 * BLOCK), :]
                scores.append(_dot_nt(qp, kb))
            outs = []
            for h, s in zip(heads, scores):
                s = s + bias_s[variant, h]
                sink = sinks_ref[h]
                m = jnp.maximum(jnp.max(s, axis=-1, keepdims=True), sink)
                e = jnp.exp(s - m)
                denom = jnp.sum(e, axis=-1, keepdims=True) + jnp.exp(sink - m)
                vb = v2_s[kh, h % 2, pl.ds(row, 2 * BLOCK), :]
                outs.append(_dot(e.astype(BF16), vb) * (1.0 / denom))
            for i in range(group // 2):
                p = (kh * group) // 2 + i
                o_pair = outs[2 * i] + outs[2 * i + 1]
                oh_s[pl.ds(row, BLOCK), p * LANES:(p + 1) * LANES] = o_pair.astype(BF16)
        return carry

    lax.fori_loop(0, n_blocks, block_body, 0)

    k2_s[:, :, 0:BLOCK, :] = k2_s[:, :, tm:tm + BLOCK, :]
    v2_s[:, :, 0:BLOCK, :] = v2_s[:, :, tm:tm + BLOCK, :]

    o_ref[...] = x + _dot(oh_s[...], wo_ref[...]) + bo_ref[...]


def _swa_mixer(x, g, w_qkv, b_qkv, sinks, w_o, b_o, *, seq):
    n, d = x.shape
    kern = functools.partial(_swa_kernel, tiles_per_seq=seq // TM)
    q_dim = ATTN_HEADS * HEAD_DIM
    return pl.pallas_call(
        kern,
        out_shape=jax.ShapeDtypeStruct((n, d), F32),
        grid=(n // TM,),
        in_specs=[
            pl.BlockSpec(memory_space=pltpu.SMEM),
            pl.BlockSpec((TM, d), lambda i: (i, 0)),
            _resident((1, d)),
            _resident(w_qkv.shape),
            _resident(b_qkv.shape),
            _resident(w_o.shape),
            _resident(b_o.shape),
        ],
        out_specs=pl.BlockSpec((TM, d), lambda i: (i, 0)),
        scratch_shapes=[
            pltpu.VMEM((TM, q_dim), BF16),
            pltpu.VMEM((ATTN_KV_HEADS, 2, BLOCK + TM, LANES), BF16),
            pltpu.VMEM((ATTN_KV_HEADS, 2, BLOCK + TM, LANES), BF16),
            pltpu.VMEM((TM, q_dim), BF16),
            pltpu.VMEM((2, ATTN_HEADS, BLOCK, 2 * BLOCK), F32),
        ],
        compiler_params=_params(),
        name="swa_mixer",
    )(sinks, x, g, w_qkv, b_qkv, w_o, b_o)


def _conformer_kernel(x_ref, g_ref, w1_ref, b1_ref, wdw_ref, bdw_ref, lng_ref, lnb_ref,
                      w2_ref, b2_ref, o_ref, carry_ref, u_ref, *, tiles_per_seq):
    tm = x_ref.shape[0]
    first = (pl.program_id(0) % tiles_per_seq) == 0
    x = x_ref[...]
    xn = _rms(x, g_ref[...]).astype(BF16)
    for j in range(D_MODEL // CHUNK):
        lo, hi = j * CHUNK, (j + 1) * CHUNK
        a = _dot(xn, w1_ref[:, lo:hi]) + b1_ref[:, lo:hi]
        gate = _dot(xn, w1_ref[:, D_MODEL + lo:D_MODEL + hi]) + b1_ref[:, D_MODEL + lo:D_MODEL + hi]
        u = a * _sigmoid(gate)
        prev = jnp.where(first, 0.0, carry_ref[:, lo:hi])
        carry_ref[:, lo:hi] = u[tm - CONF_CARRY:, :]
        ext = jnp.concatenate([prev, u], axis=0)
        acc = jnp.zeros((tm, CHUNK), F32) + bdw_ref[:, lo:hi]
        for r in range(SUBLANES):
            ext_r = ext if r == 0 else pltpu.roll(ext, r, 0)
            for a8 in range(CONF_CARRY // SUBLANES):
                delay = a8 * SUBLANES + r
                if delay >= CONF_WIDTH:
                    continue
                tap = CONF_WIDTH - 1 - delay
                start = CONF_CARRY - a8 * SUBLANES
                acc = acc + wdw_ref[tap:tap + 1, lo:hi] * ext_r[start:start + tm, :]
        u_ref[:, lo:hi] = acc
    u = u_ref[...]
    mu = jnp.mean(u, axis=-1, keepdims=True)
    uc = u - mu
    y = uc * lax.rsqrt(jnp.mean(uc * uc, axis=-1, keepdims=True) + EPS)
    y = y * lng_ref[...] + lnb_ref[...]
    y = (y * _sigmoid(y)).astype(BF16)
    o_ref[...] = x + _dot(y, w2_ref[...]) + b2_ref[...]


def _conformer_mixer(x, g, w1, b1, wdw, bdw, lng, lnb, w2, b2, *, seq):
    n, d = x.shape
    kern = functools.partial(_conformer_kernel, tiles_per_seq=seq // TM)
    return pl.pallas_call(
        kern,
        out_shape=jax.ShapeDtypeStruct((n, d), F32),
        grid=(n // TM,),
        in_specs=[pl.BlockSpec((TM, d), lambda i: (i, 0))]
        + [_resident(a.shape) for a in (g, w1, b1, wdw, bdw, lng, lnb, w2, b2)],
        out_specs=pl.BlockSpec((TM, d), lambda i: (i, 0)),
        scratch_shapes=[pltpu.VMEM((CONF_CARRY, d), F32),
                        pltpu.VMEM((TM, d), F32)],
        compiler_params=_params(),
        name="conformer_mixer",
    )(x, g, w1, b1, wdw, bdw, lng, lnb, w2, b2)


def _memkv_kernel(mem_ref, g_ref, wkv_ref, k_ref, v_ref):
    d = mem_ref.shape[-1]
    mn = _rms(mem_ref[...], g_ref[...]).astype(BF16)
    k_ref[...] = _dot(mn, wkv_ref[:, :d]).astype(BF16)
    v_ref[...] = _dot(mn, wkv_ref[:, d:]).astype(BF16)


def _memory_kv(mem, g_mem, w_kv):
    bsz, m, d = mem.shape
    depth = w_kv.shape[0]
    out = jax.ShapeDtypeStruct((depth, bsz, m, d), BF16)
    return pl.pallas_call(
        _memkv_kernel,
        out_shape=(out, out),
        grid=(depth, bsz),
        in_specs=[
            pl.BlockSpec((None, m, d), lambda l, b: (b, 0, 0)),
            pl.BlockSpec((None, 1, d), lambda l, b: (l, 0, 0)),
            pl.BlockSpec((None, d, 2 * d), lambda l, b: (l, 0, 0)),
        ],
        out_specs=(pl.BlockSpec((None, None, m, d), lambda l, b: (l, b, 0, 0)),
                   pl.BlockSpec((None, None, m, d), lambda l, b: (l, b, 0, 0))),
        compiler_params=pltpu.CompilerParams(
            dimension_semantics=("arbitrary", "arbitrary"),
            vmem_limit_bytes=VMEM_LIMIT_BYTES),
        name="memory_kv",
    )(mem, g_mem, w_kv)


def _xattn_kernel(x_ref, g_ref, wq_ref, k_ref, v_ref, wo_ref, o_ref, q_s, oh_s):
    hd = XATTN_HEAD_DIM
    x = x_ref[...]
    xn = _rms(x, g_ref[...]).astype(BF16)
    q_s[...] = (_dot(xn, wq_ref[...]) * (hd ** -0.5)).astype(BF16)
    scores = [_dot_nt(q_s[:, h * hd:(h + 1) * hd], k_ref[:, h * hd:(h + 1) * hd])
              for h in range(XATTN_HEADS)]
    for h, s in enumerate(scores):
        lo, hi = h * hd, (h + 1) * hd
        m = jnp.max(s, axis=-1, keepdims=True)
        e = jnp.exp(s - m)
        denom = jnp.sum(e, axis=-1, keepdims=True)
        o = _dot(e.astype(BF16), v_ref[:, lo:hi]) * (1.0 / denom)
        oh_s[:, lo:hi] = o.astype(BF16)
    o_ref[...] = x + _dot(oh_s[...], wo_ref[...])


def _memory_xattn(x, g, w_q, k, v, w_o, *, seq):
    n, d = x.shape
    m = k.shape[1]
    tiles_per_seq = seq // TM
    return pl.pallas_call(
        _xattn_kernel,
        out_shape=jax.ShapeDtypeStruct((n, d), F32),
        grid=(n // TM,),
        in_specs=[
            pl.BlockSpec((TM, d), lambda i: (i, 0)),
            _resident((1, d)),
            _resident(w_q.shape),
            pl.BlockSpec((None, m, d), lambda i: (i // tiles_per_seq, 0, 0)),
            pl.BlockSpec((None, m, d), lambda i: (i // tiles_per_seq, 0, 0)),
            _resident(w_o.shape),
        ],
        out_specs=pl.BlockSpec((TM, d), lambda i: (i, 0)),
        scratch_shapes=[pltpu.VMEM((TM, d), BF16), pltpu.VMEM((TM, d), BF16)],
        compiler_params=_params(),
        name="memory_xattn",
    )(x, g, w_q, k, v, w_o)


def _ffn_kernel(x_ref, g_ref, wup_ref, cw_ref, wd_ref, gf_ref, o_ref,
                act_s, carry_s, *, tiles_per_seq, final_norm):
    tm = x_ref.shape[0]
    first = (pl.program_id(0) % tiles_per_seq) == 0
    x = x_ref[...]
    xn = _rms(x, g_ref[...]).astype(BF16)
    for j in range(D_FF // CHUNK):
        lo, hi = j * CHUNK, (j + 1) * CHUNK
        gate = _dot(xn, wup_ref[:, lo:hi])
        up = _dot(xn, wup_ref[:, D_FF + lo:D_FF + hi])
        prev = jnp.where(first, 0.0, carry_s[:, lo:hi])
        carry_s[:, lo:hi] = gate[tm - SUBLANES:, :]
        cv = (cw_ref[2:3, lo:hi] * gate
              + cw_ref[1:2, lo:hi] * _shifted(prev, gate, 1)
              + cw_ref[0:1, lo:hi] * _shifted(prev, gate, 2))
        act_s[:, lo:hi] = (cv * _sigmoid(cv) * up).astype(BF16)
    y = x + _dot(act_s[...], wd_ref[...])
    if final_norm:
        y = _rms(y, gf_ref[...])
    o_ref[...] = y


def _conv_gated_mlp(x, g, w_up, cw, wd, g_final, *, seq, final_norm):
    n, d = x.shape
    kern = functools.partial(_ffn_kernel, tiles_per_seq=seq // TM, final_norm=final_norm)
    return pl.pallas_call(
        kern,
        out_shape=jax.ShapeDtypeStruct((n, d), F32),
        grid=(n // TM,),
        in_specs=[pl.BlockSpec((TM, d), lambda i: (i, 0))]
        + [_resident(a.shape) for a in (g, w_up, cw, wd, g_final)],
        out_specs=pl.BlockSpec((TM, d), lambda i: (i, 0)),
        scratch_shapes=[pltpu.VMEM((TM, D_FF), BF16),
                        pltpu.VMEM((SUBLANES, D_FF), F32)],
        compiler_params=_params(),
        name="conv_gated_mlp",
    )(x, g, w_up, cw, wd, g_final)


def kernel(x, mem, norm_mix, norm_xattn, norm_mem, norm_ffn, norm_final, a_w_in, a_conv, a_w_out, b_w_qkv, b_b_qkv, b_sinks, b_w_o, b_b_o, c_w_pw1, c_b_pw1, c_w_dw, c_b_dw, c_ln_g, c_ln_b, c_w_pw2, c_b_pw2, x_w_q, x_w_kv, x_w_o, f_w_up, f_conv, f_w_down):
    bsz, seq, d = x.shape
    depth = norm_mix.shape[0]
    assert d == D_MODEL and seq % TM == 0 and TM % BLOCK == 0
    n_chunks = D_FF // CHUNK

    h = x.reshape(bsz * seq, d)
    k_all, v_all = _memory_kv(mem, norm_mem.reshape(depth, 1, d), x_w_kv.astype(BF16))

    for i in range(depth):
        kind, j = i % 3, i // 3
        g_mix = norm_mix[i].reshape(1, d)
        if kind == 0:
            h = _sconv_mixer(h, g_mix, a_w_in[j].astype(BF16), a_conv[j],
                             a_w_out[j].astype(BF16), seq=seq)
        elif kind == 1:
            h = _swa_mixer(h, g_mix, b_w_qkv[j].astype(BF16), b_b_qkv[j].reshape(1, -1),
                           b_sinks[j], b_w_o[j].astype(BF16), b_b_o[j].reshape(1, d), seq=seq)
        else:
            h = _conformer_mixer(h, g_mix, c_w_pw1[j].astype(BF16), c_b_pw1[j].reshape(1, -1),
                                 c_w_dw[j], c_b_dw[j].reshape(1, d), c_ln_g[j].reshape(1, d),
                                 c_ln_b[j].reshape(1, d), c_w_pw2[j].astype(BF16),
                                 c_b_pw2[j].reshape(1, d), seq=seq)
        h = _memory_xattn(h, norm_xattn[i].reshape(1, d), x_w_q[i].astype(BF16),
                          k_all[i], v_all[i], x_w_o[i].astype(BF16), seq=seq)
        h = _conv_gated_mlp(h, norm_ffn[i].reshape(1, d), f_w_up[i].astype(BF16), f_conv[i],
                            f_w_down[i].astype(BF16), norm_final.reshape(1, d), seq=seq,
                            final_norm=(i == depth - 1))
    return h.reshape(bsz, seq, d)
```

```python
import functools

import jax
import jax.numpy as jnp
from jax import lax
from jax.experimental import pallas as pl
from jax.experimental.pallas import tpu as pltpu

EPS = 1e-6
F32 = jnp.float32
BF16 = jnp.bfloat16

D_MODEL = 1024
SC_WIDTH = 3
ATTN_HEADS = 16
ATTN_KV_HEADS = 4
HEAD_DIM = 64
WINDOW = 128
BLOCK = 128
CONF_WIDTH = 31
XATTN_HEADS = 4
XATTN_HEAD_DIM = D_MODEL // XATTN_HEADS
D_FF = 2816
FFN_CONV_WIDTH = 3

SUBLANES = 8
LANES = 128
MXU_DIM = 256
VMEM_LIMIT_BYTES = 56 * 1024 * 1024

TM = 512
CHUNK = MXU_DIM
CONF_CARRY = 32


def _rms(x, g):
    ms = jnp.mean(x * x, axis=-1, keepdims=True)
    return x * lax.rsqrt(ms + EPS) * g


def _dot(a, b):
    return jnp.dot(a, b, preferred_element_type=F32)


def _dot_nt(a, b):
    return lax.dot_general(a, b, (((1,), (1,)), ((), ())), preferred_element_type=F32)


def _sigmoid(x):
    return 1.0 / (1.0 + jnp.exp(-x))


def _shifted(prev, cur, shift):
    n_prev = prev.shape[0]
    ext = jnp.concatenate([prev, cur], axis=0)
    return pltpu.roll(ext, shift, 0)[n_prev:]


def _to_bf16(src_ref, dst_ref, rows=256):
    for r in range(0, src_ref.shape[0], rows):
        dst_ref[r:r + rows, :] = src_ref[r:r + rows, :].astype(BF16)


def _resident(shape):
    nd = len(shape)
    return pl.BlockSpec(shape, lambda *_: (0,) * nd, pipeline_mode=pl.Buffered(1))


def _params():
    return pltpu.CompilerParams(dimension_semantics=("arbitrary",),
                                vmem_limit_bytes=VMEM_LIMIT_BYTES)


def _sconv_kernel(x_ref, g_ref, win_f32, cw_ref, wout_f32, o_ref, carry_ref, z_ref,
                  win_ref, wout_ref, *, tiles_per_seq):
    tm = x_ref.shape[0]

    @pl.when(pl.program_id(0) == 0)
    def _():
        _to_bf16(win_f32, win_ref)
        _to_bf16(wout_f32, wout_ref)

    first = (pl.program_id(0) % tiles_per_seq) == 0
    x = x_ref[...]
    xn = _rms(x, g_ref[...]).astype(BF16)
    cw = cw_ref[...]
    for j in range(D_MODEL // CHUNK):
        lo, hi = j * CHUNK, (j + 1) * CHUNK
        b = _dot(xn, win_ref[:, lo:hi])
        c = _dot(xn, win_ref[:, D_MODEL + lo:D_MODEL + hi])
        v = _dot(xn, win_ref[:, 2 * D_MODEL + lo:2 * D_MODEL + hi])
        u = c * v
        prev = jnp.where(first, 0.0, carry_ref[:, lo:hi])
        carry_ref[:, lo:hi] = u[tm - SUBLANES:, :]
        conv = (cw[2:3, lo:hi] * u
                + cw[1:2, lo:hi] * _shifted(prev, u, 1)
                + cw[0:1, lo:hi] * _shifted(prev, u, 2))
        z_ref[:, lo:hi] = (b * conv).astype(BF16)
    o_ref[...] = x + _dot(z_ref[...], wout_ref[...])


def _sconv_mixer(x, g, w_in, conv_w, w_out, *, seq):
    n, d = x.shape
    kern = functools.partial(_sconv_kernel, tiles_per_seq=seq // TM)
    return pl.pallas_call(
        kern,
        out_shape=jax.ShapeDtypeStruct((n, d), F32),
        grid=(n // TM,),
        in_specs=[
            pl.BlockSpec((TM, d), lambda i: (i, 0)),
            _resident((1, d)),
            _resident(w_in.shape),
            _resident(conv_w.shape),
            _resident(w_out.shape),
        ],
        out_specs=pl.BlockSpec((TM, d), lambda i: (i, 0)),
        scratch_shapes=[pltpu.VMEM((SUBLANES, d), F32),
                        pltpu.VMEM((TM, d), BF16),
                        pltpu.VMEM(w_in.shape, BF16),
                        pltpu.VMEM(w_out.shape, BF16)],
        compiler_params=_params(),
        name="sconv_mixer",
    )(x, g, w_in, conv_w, w_out)


def _swa_kernel(sinks_ref, x_ref, g_ref, wqkv_f32, bqkv_ref, wo_f32, bo_ref, o_ref,
                q_s, k2_s, v2_s, oh_s, bias_s, wqkv_ref, wo_ref, *, tiles_per_seq):
    tm = x_ref.shape[0]
    n_blocks = tm // BLOCK
    group = ATTN_HEADS // ATTN_KV_HEADS
    q_dim = ATTN_HEADS * HEAD_DIM
    kv_dim = ATTN_KV_HEADS * HEAD_DIM
    first = (pl.program_id(0) % tiles_per_seq) == 0

    @pl.when(first)
    def _():
        k2_s[:, :, 0:BLOCK, :] = jnp.zeros((ATTN_KV_HEADS, 2, BLOCK, LANES), BF16)
        v2_s[:, :, 0:BLOCK, :] = jnp.zeros((ATTN_KV_HEADS, 2, BLOCK, LANES), BF16)

    @pl.when(pl.program_id(0) == 0)
    def _():
        _to_bf16(wqkv_f32, wqkv_ref)
        _to_bf16(wo_f32, wo_ref)
        qi = lax.broadcasted_iota(jnp.int32, (BLOCK, 2 * BLOCK), 0)
        kj = lax.broadcasted_iota(jnp.int32, (BLOCK, 2 * BLOCK), 1)
        dist = BLOCK + qi - kj
        neg_inf = jnp.float32(-jnp.inf)
        pen = jnp.where((dist >= 0) & (dist < WINDOW), 0.0, neg_inf)
        pen_start = jnp.where(kj < BLOCK, neg_inf, pen)
        distf = dist.astype(F32)
        for h in range(ATTN_HEADS):
            slope = 2.0 ** (-8.0 * (h + 1) / ATTN_HEADS)
            bias_s[0, h] = pen - slope * distf
            bias_s[1, h] = pen_start - slope * distf

    x = x_ref[...]
    xn = _rms(x, g_ref[...]).astype(BF16)
    scale = HEAD_DIM ** -0.5
    for j in range(q_dim // CHUNK):
        lo, hi = j * CHUNK, (j + 1) * CHUNK
        q_s[:, lo:hi] = ((_dot(xn, wqkv_ref[:, lo:hi]) + bqkv_ref[:, lo:hi]) * scale).astype(BF16)

    lane = lax.broadcasted_iota(jnp.int32, (tm, LANES), 1)
    for (dst, base) in ((k2_s, q_dim), (v2_s, q_dim + kv_dim)):
        for gi in range(kv_dim // LANES):
            lo = base + gi * LANES
            kg = _dot(xn, wqkv_ref[:, lo:lo + LANES]) + bqkv_ref[:, lo:lo + LANES]
            low = jnp.where(lane < HEAD_DIM, kg, 0.0)
            high = jnp.where(lane >= HEAD_DIM, kg, 0.0)
            low_hi = pltpu.roll(low, HEAD_DIM, 1)
            high_lo = pltpu.roll(high, HEAD_DIM, 1)
            dst[2 * gi, 0, BLOCK:, :] = low.astype(BF16)
            dst[2 * gi, 1, BLOCK:, :] = low_hi.astype(BF16)
            dst[2 * gi + 1, 0, BLOCK:, :] = high_lo.astype(BF16)
            dst[2 * gi + 1, 1, BLOCK:, :] = high.astype(BF16)

    for n in range(n_blocks):
        row = n * BLOCK
        variant = jnp.where(first, 1, 0) if n == 0 else 0
        for kh in range(ATTN_KV_HEADS):
            heads = range(kh * group, (kh + 1) * group)
            scores = []
            for h in heads:
                qp = q_s[pl.ds(row, BLOCK), (h // 2) * LANES:(h // 2 + 1) * LANES]
                kb = k2_s[kh, h % 2, pl.ds(row, 2 * BLOCK), :]
                scores.append(_dot_nt(qp, kb))
            outs = []
            for h, s in zip(heads, scores):
                s = s + bias_s[variant, h]
                sink = sinks_ref[h]
                m = jnp.maximum(jnp.max(s, axis=-1, keepdims=True), sink)
                e = jnp.exp(s - m)
                denom = jnp.sum(e, axis=-1, keepdims=True) + jnp.exp(sink - m)
                vb = v2_s[kh, h % 2, pl.ds(row, 2 * BLOCK), :]
                outs.append(_dot(e.astype(BF16), vb) * (1.0 / denom))
            for i in range(group // 2):
                p = (kh * group) // 2 + i
                o_pair = outs[2 * i] + outs[2 * i + 1]
                oh_s[pl.ds(row, BLOCK), p * LANES:(p + 1) * LANES] = o_pair.astype(BF16)

    k2_s[:, :, 0:BLOCK, :] = k2_s[:, :, tm:tm + BLOCK, :]
    v2_s[:, :, 0:BLOCK, :] = v2_s[:, :, tm:tm + BLOCK, :]

    o_ref[...] = x + _dot(oh_s[...], wo_ref[...]) + bo_ref[...]


def _swa_mixer(x, g, w_qkv, b_qkv, sinks, w_o, b_o, *, seq):
    n, d = x.shape
    kern = functools.partial(_swa_kernel, tiles_per_seq=seq // TM)
    q_dim = ATTN_HEADS * HEAD_DIM
    return pl.pallas_call(
        kern,
        out_shape=jax.ShapeDtypeStruct((n, d), F32),
        grid=(n // TM,),
        in_specs=[
            pl.BlockSpec(memory_space=pltpu.SMEM),
            pl.BlockSpec((TM, d), lambda i: (i, 0)),
            _resident((1, d)),
            _resident(w_qkv.shape),
            _resident(b_qkv.shape),
            _resident(w_o.shape),
            _resident(b_o.shape),
        ],
        out_specs=pl.BlockSpec((TM, d), lambda i: (i, 0)),
        scratch_shapes=[
            pltpu.VMEM((TM, q_dim), BF16),
            pltpu.VMEM((ATTN_KV_HEADS, 2, BLOCK + TM, LANES), BF16),
            pltpu.VMEM((ATTN_KV_HEADS, 2, BLOCK + TM, LANES), BF16),
            pltpu.VMEM((TM, q_dim), BF16),
            pltpu.VMEM((2, ATTN_HEADS, BLOCK, 2 * BLOCK), F32),
            pltpu.VMEM(w_qkv.shape, BF16),
            pltpu.VMEM(w_o.shape, BF16),
        ],
        compiler_params=_params(),
        name="swa_mixer",
    )(sinks, x, g, w_qkv, b_qkv, w_o, b_o)


def _conformer_kernel(x_ref, g_ref, w1_f32, b1_ref, wdw_ref, bdw_ref, lng_ref, lnb_ref,
                      w2_f32, b2_ref, o_ref, carry_ref, u_ref, w1_ref, w2_ref, *, tiles_per_seq):
    tm = x_ref.shape[0]

    @pl.when(pl.program_id(0) == 0)
    def _():
        _to_bf16(w1_f32, w1_ref)
        _to_bf16(w2_f32, w2_ref)

    first = (pl.program_id(0) % tiles_per_seq) == 0
    x = x_ref[...]
    xn = _rms(x, g_ref[...]).astype(BF16)
    for j in range(D_MODEL // CHUNK):
        lo, hi = j * CHUNK, (j + 1) * CHUNK
        a = _dot(xn, w1_ref[:, lo:hi]) + b1_ref[:, lo:hi]
        gate = _dot(xn, w1_ref[:, D_MODEL + lo:D_MODEL + hi]) + b1_ref[:, D_MODEL + lo:D_MODEL + hi]
        u = a * _sigmoid(gate)
        prev = jnp.where(first, 0.0, carry_ref[:, lo:hi])
        carry_ref[:, lo:hi] = u[tm - CONF_CARRY:, :]
        ext = jnp.concatenate([prev, u], axis=0)
        acc = jnp.zeros((tm, CHUNK), F32) + bdw_ref[:, lo:hi]
        for r in range(SUBLANES):
            ext_r = ext if r == 0 else pltpu.roll(ext, r, 0)
            for a8 in range(CONF_CARRY // SUBLANES):
                delay = a8 * SUBLANES + r
                if delay >= CONF_WIDTH:
                    continue
                tap = CONF_WIDTH - 1 - delay
                start = CONF_CARRY - a8 * SUBLANES
                acc = acc + wdw_ref[tap:tap + 1, lo:hi] * ext_r[start:start + tm, :]
        u_ref[:, lo:hi] = acc
    u = u_ref[...]
    mu = jnp.mean(u, axis=-1, keepdims=True)
    uc = u - mu
    y = uc * lax.rsqrt(jnp.mean(uc * uc, axis=-1, keepdims=True) + EPS)
    y = y * lng_ref[...] + lnb_ref[...]
    y = (y * _sigmoid(y)).astype(BF16)
    o_ref[...] = x + _dot(y, w2_ref[...]) + b2_ref[...]


def _conformer_mixer(x, g, w1, b1, wdw, bdw, lng, lnb, w2, b2, *, seq):
    n, d = x.shape
    kern = functools.partial(_conformer_kernel, tiles_per_seq=seq // TM)
    return pl.pallas_call(
        kern,
        out_shape=jax.ShapeDtypeStruct((n, d), F32),
        grid=(n // TM,),
        in_specs=[pl.BlockSpec((TM, d), lambda i: (i, 0))]
        + [_resident(a.shape) for a in (g, w1, b1, wdw, bdw, lng, lnb, w2, b2)],
        out_specs=pl.BlockSpec((TM, d), lambda i: (i, 0)),
        scratch_shapes=[pltpu.VMEM((CONF_CARRY, d), F32),
                        pltpu.VMEM((TM, d), F32),
                        pltpu.VMEM(w1.shape, BF16),
                        pltpu.VMEM(w2.shape, BF16)],
        compiler_params=_params(),
        name="conformer_mixer",
    )(x, g, w1, b1, wdw, bdw, lng, lnb, w2, b2)


def _memkv_kernel(mem_ref, g_ref, wkv_ref, k_ref, v_ref):
    d = mem_ref.shape[-1]
    mn = _rms(mem_ref[...], g_ref[...]).astype(BF16)
    k_ref[...] = _dot(mn, wkv_ref[:, :d].astype(BF16)).astype(BF16)
    v_ref[...] = _dot(mn, wkv_ref[:, d:].astype(BF16)).astype(BF16)


def _memory_kv(mem, g_mem, w_kv):
    bsz, m, d = mem.shape
    depth = w_kv.shape[0]
    out = jax.ShapeDtypeStruct((depth, bsz, m, d), BF16)
    return pl.pallas_call(
        _memkv_kernel,
        out_shape=(out, out),
        grid=(depth, bsz),
        in_specs=[
            pl.BlockSpec((None, m, d), lambda l, b: (b, 0, 0)),
            pl.BlockSpec((None, 1, d), lambda l, b: (l, 0, 0)),
            pl.BlockSpec((None, d, 2 * d), lambda l, b: (l, 0, 0)),
        ],
        out_specs=(pl.BlockSpec((None, None, m, d), lambda l, b: (l, b, 0, 0)),
                   pl.BlockSpec((None, None, m, d), lambda l, b: (l, b, 0, 0))),
        compiler_params=pltpu.CompilerParams(
            dimension_semantics=("arbitrary", "arbitrary"),
            vmem_limit_bytes=VMEM_LIMIT_BYTES),
        name="memory_kv",
    )(mem, g_mem, w_kv)


def _xattn_kernel(x_ref, g_ref, wq_f32, k_ref, v_ref, wo_f32, o_ref, q_s, oh_s, wq_ref, wo_ref):
    hd = XATTN_HEAD_DIM

    @pl.when(pl.program_id(0) == 0)
    def _():
        _to_bf16(wq_f32, wq_ref)
        _to_bf16(wo_f32, wo_ref)

    x = x_ref[...]
    xn = _rms(x, g_ref[...]).astype(BF16)
    q_s[...] = (_dot(xn, wq_ref[...]) * (hd ** -0.5)).astype(BF16)
    scores = [_dot_nt(q_s[:, h * hd:(h + 1) * hd], k_ref[:, h * hd:(h + 1) * hd])
              for h in range(XATTN_HEADS)]
    for h, s in enumerate(scores):
        lo, hi = h * hd, (h + 1) * hd
        m = jnp.max(s, axis=-1, keepdims=True)
        e = jnp.exp(s - m)
        denom = jnp.sum(e, axis=-1, keepdims=True)
        o = _dot(e.astype(BF16), v_ref[:, lo:hi]) * (1.0 / denom)
        oh_s[:, lo:hi] = o.astype(BF16)
    o_ref[...] = x + _dot(oh_s[...], wo_ref[...])


def _memory_xattn(x, g, w_q, k, v, w_o, *, seq):
    n, d = x.shape
    m = k.shape[1]
    tiles_per_seq = seq // TM
    return pl.pallas_call(
        _xattn_kernel,
        out_shape=jax.ShapeDtypeStruct((n, d), F32),
        grid=(n // TM,),
        in_specs=[
            pl.BlockSpec((TM, d), lambda i: (i, 0)),
            _resident((1, d)),
            _resident(w_q.shape),
            pl.BlockSpec((None, m, d), lambda i: (i // tiles_per_seq, 0, 0)),
            pl.BlockSpec((None, m, d), lambda i: (i // tiles_per_seq, 0, 0)),
            _resident(w_o.shape),
        ],
        out_specs=pl.BlockSpec((TM, d), lambda i: (i, 0)),
        scratch_shapes=[pltpu.VMEM((TM, d), BF16), pltpu.VMEM((TM, d), BF16),
                        pltpu.VMEM(w_q.shape, BF16), pltpu.VMEM(w_o.shape, BF16)],
        compiler_params=_params(),
        name="memory_xattn",
    )(x, g, w_q, k, v, w_o)


def _ffn_kernel(x_ref, g_ref, wup_ref, cw_ref, wd_f32, gf_ref, o_ref,
                act_s, carry_s, wd_ref, *, tiles_per_seq, final_norm):
    tm = x_ref.shape[0]

    @pl.when(pl.program_id(0) == 0)
    def _():
        _to_bf16(wd_f32, wd_ref)

    first = (pl.program_id(0) % tiles_per_seq) == 0
    x = x_ref[...]
    xn = _rms(x, g_ref[...]).astype(BF16)
    for j in range(D_FF // CHUNK):
        lo, hi = j * CHUNK, (j + 1) * CHUNK
        gate = _dot(xn, wup_ref[:, lo:hi])
        up = _dot(xn, wup_ref[:, D_FF + lo:D_FF + hi])
        prev = jnp.where(first, 0.0, carry_s[:, lo:hi])
        carry_s[:, lo:hi] = gate[tm - SUBLANES:, :]
        cv = (cw_ref[2:3, lo:hi] * gate
              + cw_ref[1:2, lo:hi] * _shifted(prev, gate, 1)
              + cw_ref[0:1, lo:hi] * _shifted(prev, gate, 2))
        act_s[:, lo:hi] = (cv * _sigmoid(cv) * up).astype(BF16)
    y = x + _dot(act_s[...], wd_ref[...])
    if final_norm:
        y = _rms(y, gf_ref[...])
    o_ref[...] = y


def _conv_gated_mlp(x, g, w_up, cw, wd, g_final, *, seq, final_norm):
    n, d = x.shape
    kern = functools.partial(_ffn_kernel, tiles_per_seq=seq // TM, final_norm=final_norm)
    return pl.pallas_call(
        kern,
        out_shape=jax.ShapeDtypeStruct((n, d), F32),
        grid=(n // TM,),
        in_specs=[pl.BlockSpec((TM, d), lambda i: (i, 0))]
        + [_resident(a.shape) for a in (g, w_up, cw, wd, g_final)],
        out_specs=pl.BlockSpec((TM, d), lambda i: (i, 0)),
        scratch_shapes=[pltpu.VMEM((TM, D_FF), BF16),
                        pltpu.VMEM((SUBLANES, D_FF), F32),
                        pltpu.VMEM(wd.shape, BF16)],
        compiler_params=_params(),
        name="conv_gated_mlp",
    )(x, g, w_up, cw, wd, g_final)


def kernel(x, mem, norm_mix, norm_xattn, norm_mem, norm_ffn, norm_final, a_w_in, a_conv, a_w_out, b_w_qkv, b_b_qkv, b_sinks, b_w_o, b_b_o, c_w_pw1, c_b_pw1, c_w_dw, c_b_dw, c_ln_g, c_ln_b, c_w_pw2, c_b_pw2, x_w_q, x_w_kv, x_w_o, f_w_up, f_conv, f_w_down):
    bsz, seq, d = x.shape
    depth = norm_mix.shape[0]
    assert d == D_MODEL and seq % TM == 0 and TM % BLOCK == 0

    h = x.reshape(bsz * seq, d)
    k_all, v_all = _memory_kv(mem, norm_mem.reshape(depth, 1, d), x_w_kv)

    for i in range(depth):
        kind, j = i % 3, i // 3
        g_mix = norm_mix[i].reshape(1, d)
        if kind == 0:
            h = _sconv_mixer(h, g_mix, a_w_in[j], a_conv[j], a_w_out[j], seq=seq)
        elif kind == 1:
            h = _swa_mixer(h, g_mix, b_w_qkv[j], b_b_qkv[j].reshape(1, -1),
                           b_sinks[j], b_w_o[j], b_b_o[j].reshape(1, d), seq=seq)
        else:
            h = _conformer_mixer(h, g_mix, c_w_pw1[j], c_b_pw1[j].reshape(1, -1),
                                 c_w_dw[j], c_b_dw[j].reshape(1, d), c_ln_g[j].reshape(1, d),
                                 c_ln_b[j].reshape(1, d), c_w_pw2[j],
                                 c_b_pw2[j].reshape(1, d), seq=seq)
        h = _memory_xattn(h, norm_xattn[i].reshape(1, d), x_w_q[i],
                          k_all[i], v_all[i], x_w_o[i], seq=seq)
        h = _conv_gated_mlp(h, norm_ffn[i].reshape(1, d), f_w_up[i].astype(BF16), f_conv[i],
                            f_w_down[i], norm_final.reshape(1, d), seq=seq,
                            final_norm=(i == depth - 1))
    return h.reshape(bsz, seq, d)
```

```python
import functools

import jax
import jax.numpy as jnp
from jax import lax
from jax.experimental import pallas as pl
from jax.experimental.pallas import tpu as pltpu

EPS = 1e-6
F32 = jnp.float32
BF16 = jnp.bfloat16

D_MODEL = 1024
SC_WIDTH = 3
ATTN_HEADS = 16
ATTN_KV_HEADS = 4
HEAD_DIM = 64
WINDOW = 128
BLOCK = 128
CONF_WIDTH = 31
XATTN_HEADS = 4
XATTN_HEAD_DIM = D_MODEL // XATTN_HEADS
D_FF = 2816
FFN_CONV_WIDTH = 3

SUBLANES = 8
LANES = 128
MXU_DIM = 256
VMEM_LIMIT_BYTES = 56 * 1024 * 1024

TM = 512
CHUNK = MXU_DIM
CONF_CARRY = 32


def _rms(x, g):
    ms = jnp.mean(x * x, axis=-1, keepdims=True)
    return x * lax.rsqrt(ms + EPS) * g


def _dot(a, b):
    return jnp.dot(a, b, preferred_element_type=F32)


def _dot_nt(a, b):
    return lax.dot_general(a, b, (((1,), (1,)), ((), ())), preferred_element_type=F32)


def _sigmoid(x):
    return 1.0 / (1.0 + jnp.exp(-x))


def _shifted(prev, cur, shift):
    n_prev = prev.shape[0]
    ext = jnp.concatenate([prev, cur], axis=0)
    return pltpu.roll(ext, shift, 0)[n_prev:]


def _to_bf16(src_ref, dst_ref, rows=256):
    for r in range(0, src_ref.shape[0], rows):
        dst_ref[r:r + rows, :] = src_ref[r:r + rows, :].astype(BF16)


def _resident(shape):
    nd = len(shape)
    return pl.BlockSpec(shape, lambda *_: (0,) * nd, pipeline_mode=pl.Buffered(1))


def _layer(shape, layer):
    nd = len(shape) - 1
    return pl.BlockSpec((None,) + tuple(shape[1:]), lambda *_: (layer,) + (0,) * nd,
                        pipeline_mode=pl.Buffered(1))


def _params():
    return pltpu.CompilerParams(dimension_semantics=("arbitrary",),
                                vmem_limit_bytes=VMEM_LIMIT_BYTES)


def _sconv_kernel(x_ref, g_ref, win_f32, cw_ref, wout_f32, o_ref, carry_ref, z_ref,
                  win_ref, wout_ref, *, tiles_per_seq):
    tm = x_ref.shape[0]

    @pl.when(pl.program_id(0) == 0)
    def _():
        _to_bf16(win_f32, win_ref)
        _to_bf16(wout_f32, wout_ref)

    first = (pl.program_id(0) % tiles_per_seq) == 0
    x = x_ref[...]
    xn = _rms(x, g_ref[...]).astype(BF16)
    cw = cw_ref[...]
    for j in range(D_MODEL // CHUNK):
        lo, hi = j * CHUNK, (j + 1) * CHUNK
        b = _dot(xn, win_ref[:, lo:hi])
        c = _dot(xn, win_ref[:, D_MODEL + lo:D_MODEL + hi])
        v = _dot(xn, win_ref[:, 2 * D_MODEL + lo:2 * D_MODEL + hi])
        u = c * v
        prev = jnp.where(first, 0.0, carry_ref[:, lo:hi])
        carry_ref[:, lo:hi] = u[tm - SUBLANES:, :]
        conv = (cw[2:3, lo:hi] * u
                + cw[1:2, lo:hi] * _shifted(prev, u, 1)
                + cw[0:1, lo:hi] * _shifted(prev, u, 2))
        z_ref[:, lo:hi] = (b * conv).astype(BF16)
    o_ref[...] = x + _dot(z_ref[...], wout_ref[...])


def _sconv_mixer(x, g, w_in, conv_w, w_out, *, layer, j, seq):
    n, d = x.shape
    kern = functools.partial(_sconv_kernel, tiles_per_seq=seq // TM)
    return pl.pallas_call(
        kern,
        out_shape=jax.ShapeDtypeStruct((n, d), F32),
        grid=(n // TM,),
        in_specs=[
            pl.BlockSpec((TM, d), lambda i: (i, 0)),
            _layer(g.shape, layer),
            _layer(w_in.shape, j),
            _layer(conv_w.shape, j),
            _layer(w_out.shape, j),
        ],
        out_specs=pl.BlockSpec((TM, d), lambda i: (i, 0)),
        scratch_shapes=[pltpu.VMEM((SUBLANES, d), F32),
                        pltpu.VMEM((TM, d), BF16),
                        pltpu.VMEM(w_in.shape[1:], BF16),
                        pltpu.VMEM(w_out.shape[1:], BF16)],
        compiler_params=_params(),
        name="sconv_mixer",
    )(x, g, w_in, conv_w, w_out)


def _swa_kernel(sinks_ref, x_ref, g_ref, wqkv_f32, bqkv_ref, wo_f32, bo_ref, o_ref,
                q_s, k2_s, v2_s, oh_s, bias_s, wqkv_ref, wo_ref, *, tiles_per_seq):
    tm = x_ref.shape[0]
    n_blocks = tm // BLOCK
    group = ATTN_HEADS // ATTN_KV_HEADS
    q_dim = ATTN_HEADS * HEAD_DIM
    kv_dim = ATTN_KV_HEADS * HEAD_DIM
    first = (pl.program_id(0) % tiles_per_seq) == 0

    @pl.when(first)
    def _():
        k2_s[:, :, 0:BLOCK, :] = jnp.zeros((ATTN_KV_HEADS, 2, BLOCK, LANES), BF16)
        v2_s[:, :, 0:BLOCK, :] = jnp.zeros((ATTN_KV_HEADS, 2, BLOCK, LANES), BF16)

    @pl.when(pl.program_id(0) == 0)
    def _():
        _to_bf16(wqkv_f32, wqkv_ref)
        _to_bf16(wo_f32, wo_ref)
        qi = lax.broadcasted_iota(jnp.int32, (BLOCK, 2 * BLOCK), 0)
        kj = lax.broadcasted_iota(jnp.int32, (BLOCK, 2 * BLOCK), 1)
        dist = BLOCK + qi - kj
        neg_inf = jnp.float32(-jnp.inf)
        pen = jnp.where((dist >= 0) & (dist < WINDOW), 0.0, neg_inf)
        pen_start = jnp.where(kj < BLOCK, neg_inf, pen)
        distf = dist.astype(F32)
        for h in range(ATTN_HEADS):
            slope = 2.0 ** (-8.0 * (h + 1) / ATTN_HEADS)
            bias_s[0, h] = pen - slope * distf
            bias_s[1, h] = pen_start - slope * distf

    x = x_ref[...]
    xn = _rms(x, g_ref[...]).astype(BF16)
    scale = HEAD_DIM ** -0.5
    for j in range(q_dim // CHUNK):
        lo, hi = j * CHUNK, (j + 1) * CHUNK
        q_s[:, lo:hi] = ((_dot(xn, wqkv_ref[:, lo:hi]) + bqkv_ref[:, lo:hi]) * scale).astype(BF16)

    lane = lax.broadcasted_iota(jnp.int32, (tm, LANES), 1)
    for (dst, base) in ((k2_s, q_dim), (v2_s, q_dim + kv_dim)):
        for gi in range(kv_dim // LANES):
            lo = base + gi * LANES
            kg = _dot(xn, wqkv_ref[:, lo:lo + LANES]) + bqkv_ref[:, lo:lo + LANES]
            low = jnp.where(lane < HEAD_DIM, kg, 0.0)
            high = jnp.where(lane >= HEAD_DIM, kg, 0.0)
            low_hi = pltpu.roll(low, HEAD_DIM, 1)
            high_lo = pltpu.roll(high, HEAD_DIM, 1)
            dst[2 * gi, 0, BLOCK:, :] = low.astype(BF16)
            dst[2 * gi, 1, BLOCK:, :] = low_hi.astype(BF16)
            dst[2 * gi + 1, 0, BLOCK:, :] = high_lo.astype(BF16)
            dst[2 * gi + 1, 1, BLOCK:, :] = high.astype(BF16)

    for n in range(n_blocks):
        row = n * BLOCK
        variant = jnp.where(first, 1, 0) if n == 0 else 0
        for kh in range(ATTN_KV_HEADS):
            heads = range(kh * group, (kh + 1) * group)
            scores = []
            for h in heads:
                qp = q_s[pl.ds(row, BLOCK), (h // 2) * LANES:(h // 2 + 1) * LANES]
                kb = k2_s[kh, h % 2, pl.ds(row, 2 * BLOCK), :]
                scores.append(_dot_nt(qp, kb))
            outs = []
            for h, s in zip(heads, scores):
                s = s + bias_s[variant, h]
                sink = sinks_ref[h]
                m = jnp.maximum(jnp.max(s, axis=-1, keepdims=True), sink)
                e = jnp.exp(s - m)
                denom = jnp.sum(e, axis=-1, keepdims=True) + jnp.exp(sink - m)
                vb = v2_s[kh, h % 2, pl.ds(row, 2 * BLOCK), :]
                outs.append(_dot(e.astype(BF16), vb) * (1.0 / denom))
            for i in range(group // 2):
                p = (kh * group) // 2 + i
                o_pair = outs[2 * i] + outs[2 * i + 1]
                oh_s[pl.ds(row, BLOCK), p * LANES:(p + 1) * LANES] = o_pair.astype(BF16)

    k2_s[:, :, 0:BLOCK, :] = k2_s[:, :, tm:tm + BLOCK, :]
    v2_s[:, :, 0:BLOCK, :] = v2_s[:, :, tm:tm + BLOCK, :]

    o_ref[...] = x + _dot(oh_s[...], wo_ref[...]) + bo_ref[...]


def _swa_mixer(x, g, w_qkv, b_qkv, sinks, w_o, b_o, *, layer, j, seq):
    n, d = x.shape
    kern = functools.partial(_swa_kernel, tiles_per_seq=seq // TM)
    q_dim = ATTN_HEADS * HEAD_DIM
    return pl.pallas_call(
        kern,
        out_shape=jax.ShapeDtypeStruct((n, d), F32),
        grid=(n // TM,),
        in_specs=[
            pl.BlockSpec(memory_space=pltpu.SMEM),
            pl.BlockSpec((TM, d), lambda i: (i, 0)),
            _layer(g.shape, layer),
            _layer(w_qkv.shape, j),
            _layer(b_qkv.shape, j),
            _layer(w_o.shape, j),
            _layer(b_o.shape, j),
        ],
        out_specs=pl.BlockSpec((TM, d), lambda i: (i, 0)),
        scratch_shapes=[
            pltpu.VMEM((TM, q_dim), BF16),
            pltpu.VMEM((ATTN_KV_HEADS, 2, BLOCK + TM, LANES), BF16),
            pltpu.VMEM((ATTN_KV_HEADS, 2, BLOCK + TM, LANES), BF16),
            pltpu.VMEM((TM, q_dim), BF16),
            pltpu.VMEM((2, ATTN_HEADS, BLOCK, 2 * BLOCK), F32),
            pltpu.VMEM(w_qkv.shape[1:], BF16),
            pltpu.VMEM(w_o.shape[1:], BF16),
        ],
        compiler_params=_params(),
        name="swa_mixer",
    )(sinks, x, g, w_qkv, b_qkv, w_o, b_o)


def _conformer_kernel(x_ref, g_ref, w1_f32, b1_ref, wdw_ref, bdw_ref, lng_ref, lnb_ref,
                      w2_f32, b2_ref, o_ref, carry_ref, u_ref, w1_ref, w2_ref, *, tiles_per_seq):
    tm = x_ref.shape[0]

    @pl.when(pl.program_id(0) == 0)
    def _():
        _to_bf16(w1_f32, w1_ref)
        _to_bf16(w2_f32, w2_ref)

    first = (pl.program_id(0) % tiles_per_seq) == 0
    x = x_ref[...]
    xn = _rms(x, g_ref[...]).astype(BF16)
    for j in range(D_MODEL // CHUNK):
        lo, hi = j * CHUNK, (j + 1) * CHUNK
        a = _dot(xn, w1_ref[:, lo:hi]) + b1_ref[:, lo:hi]
        gate = _dot(xn, w1_ref[:, D_MODEL + lo:D_MODEL + hi]) + b1_ref[:, D_MODEL + lo:D_MODEL + hi]
        u = a * _sigmoid(gate)
        prev = jnp.where(first, 0.0, carry_ref[:, lo:hi])
        carry_ref[:, lo:hi] = u[tm - CONF_CARRY:, :]
        ext = jnp.concatenate([prev, u], axis=0)
        acc = jnp.zeros((tm, CHUNK), F32) + bdw_ref[:, lo:hi]
        for r in range(SUBLANES):
            ext_r = ext if r == 0 else pltpu.roll(ext, r, 0)
            for a8 in range(CONF_CARRY // SUBLANES):
                delay = a8 * SUBLANES + r
                if delay >= CONF_WIDTH:
                    continue
                tap = CONF_WIDTH - 1 - delay
                start = CONF_CARRY - a8 * SUBLANES
                acc = acc + wdw_ref[tap:tap + 1, lo:hi] * ext_r[start:start + tm, :]
        u_ref[:, lo:hi] = acc
    u = u_ref[...]
    mu = jnp.mean(u, axis=-1, keepdims=True)
    uc = u - mu
    y = uc * lax.rsqrt(jnp.mean(uc * uc, axis=-1, keepdims=True) + EPS)
    y = y * lng_ref[...] + lnb_ref[...]
    y = (y * _sigmoid(y)).astype(BF16)
    o_ref[...] = x + _dot(y, w2_ref[...]) + b2_ref[...]


def _conformer_mixer(x, g, w1, b1, wdw, bdw, lng, lnb, w2, b2, *, layer, j, seq):
    n, d = x.shape
    kern = functools.partial(_conformer_kernel, tiles_per_seq=seq // TM)
    return pl.pallas_call(
        kern,
        out_shape=jax.ShapeDtypeStruct((n, d), F32),
        grid=(n // TM,),
        in_specs=[pl.BlockSpec((TM, d), lambda i: (i, 0))]
        + [_layer(g.shape, layer)]
        + [_layer(a.shape, j) for a in (w1, b1, wdw, bdw, lng, lnb, w2, b2)],
        out_specs=pl.BlockSpec((TM, d), lambda i: (i, 0)),
        scratch_shapes=[pltpu.VMEM((CONF_CARRY, d), F32),
                        pltpu.VMEM((TM, d), F32),
                        pltpu.VMEM(w1.shape[1:], BF16),
                        pltpu.VMEM(w2.shape[1:], BF16)],
        compiler_params=_params(),
        name="conformer_mixer",
    )(x, g, w1, b1, wdw, bdw, lng, lnb, w2, b2)


def _memfold_kernel(mem_ref, g_ref, wkv_ref, wq_ref, wo_ref, m_ref, vw_ref):
    d = mem_ref.shape[-1]
    hd = XATTN_HEAD_DIM
    mn = _rms(mem_ref[...], g_ref[...]).astype(BF16)
    k = _dot(mn, wkv_ref[:, :d].astype(BF16)).astype(BF16)
    v = _dot(mn, wkv_ref[:, d:].astype(BF16)).astype(BF16)
    scale = hd ** -0.5
    for h in range(XATTN_HEADS):
        lo, hi = h * hd, (h + 1) * hd
        m_ref[:, lo:hi] = (_dot_nt(wq_ref[:, lo:hi].astype(BF16), k[:, lo:hi]) * scale).astype(BF16)
        vw_ref[lo:hi, :] = _dot(v[:, lo:hi], wo_ref[lo:hi, :].astype(BF16)).astype(BF16)


def _memory_fold(mem, g_mem, w_kv, w_q, w_o):
    bsz, m, d = mem.shape
    depth = w_kv.shape[0]
    hm = XATTN_HEADS * m
    return pl.pallas_call(
        _memfold_kernel,
        out_shape=(jax.ShapeDtypeStruct((depth, bsz, d, hm), BF16),
                   jax.ShapeDtypeStruct((depth, bsz, hm, d), BF16)),
        grid=(depth, bsz),
        in_specs=[
            pl.BlockSpec((None, m, d), lambda l, b: (b, 0, 0)),
            pl.BlockSpec((None, 1, d), lambda l, b: (l, 0, 0)),
            pl.BlockSpec((None, d, 2 * d), lambda l, b: (l, 0, 0)),
            pl.BlockSpec((None, d, d), lambda l, b: (l, 0, 0)),
            pl.BlockSpec((None, d, d), lambda l, b: (l, 0, 0)),
        ],
        out_specs=(pl.BlockSpec((None, None, d, hm), lambda l, b: (l, b, 0, 0)),
                   pl.BlockSpec((None, None, hm, d), lambda l, b: (l, b, 0, 0))),
        compiler_params=pltpu.CompilerParams(
            dimension_semantics=("arbitrary", "arbitrary"),
            vmem_limit_bytes=VMEM_LIMIT_BYTES),
        name="memory_fold",
    )(mem, g_mem, w_kv, w_q, w_o)


def _xattn_kernel(x_ref, g_ref, m_ref, vw_ref, o_ref, p_s):
    n_mem = m_ref.shape[-1] // XATTN_HEADS
    x = x_ref[...]
    xn = _rms(x, g_ref[...]).astype(BF16)
    scores = [_dot(xn, m_ref[:, h * n_mem:(h + 1) * n_mem]) for h in range(XATTN_HEADS)]
    for h, s in enumerate(scores):
        e = jnp.exp(s - jnp.max(s, axis=-1, keepdims=True))
        inv = 1.0 / jnp.sum(e, axis=-1, keepdims=True)
        p_s[:, h * n_mem:(h + 1) * n_mem] = (e * inv).astype(BF16)
    o_ref[...] = x + _dot(p_s[...], vw_ref[...])


def _memory_xattn(x, g, m_all, vw_all, *, layer, seq):
    n, d = x.shape
    hm = m_all.shape[-1]
    tiles_per_seq = seq // TM
    return pl.pallas_call(
        _xattn_kernel,
        out_shape=jax.ShapeDtypeStruct((n, d), F32),
        grid=(n // TM,),
        in_specs=[
            pl.BlockSpec((TM, d), lambda i: (i, 0)),
            _layer(g.shape, layer),
            pl.BlockSpec((None, None, d, hm), lambda i: (layer, i // tiles_per_seq, 0, 0)),
            pl.BlockSpec((None, None, hm, d), lambda i: (layer, i // tiles_per_seq, 0, 0)),
        ],
        out_specs=pl.BlockSpec((TM, d), lambda i: (i, 0)),
        scratch_shapes=[pltpu.VMEM((TM, hm), BF16)],
        compiler_params=_params(),
        name="memory_xattn",
    )(x, g, m_all, vw_all)


FFN_PREP = D_FF // CHUNK
UP_SLAB = 2 * D_FF // FFN_PREP


def _up_cols(ref, col):
    slab, off = divmod(col, UP_SLAB)
    return ref[slab, :, off:off + CHUNK]


def _ffn_kernel(x_ref, g_ref, wup_slab, cw_ref, wd_slab, gf_ref, o_ref,
                act_s, carry_s, wup_s, wd_s, *, tiles_per_seq, final_norm):
    tm = x_ref.shape[0]
    step = pl.program_id(0)

    @pl.when(step < FFN_PREP)
    def _():
        wup_s[step] = wup_slab[...].astype(BF16)
        row = pl.multiple_of(step * CHUNK, CHUNK)
        wd_s[pl.ds(row, CHUNK), :] = wd_slab[...].astype(BF16)

    @pl.when(step >= FFN_PREP)
    def _():
        first = ((step - FFN_PREP) % tiles_per_seq) == 0
        x = x_ref[...]
        xn = _rms(x, g_ref[...]).astype(BF16)
        for j in range(D_FF // CHUNK):
            lo, hi = j * CHUNK, (j + 1) * CHUNK
            gate = _dot(xn, _up_cols(wup_s, lo))
            up = _dot(xn, _up_cols(wup_s, D_FF + lo))
            prev = jnp.where(first, 0.0, carry_s[:, lo:hi])
            carry_s[:, lo:hi] = gate[tm - SUBLANES:, :]
            cv = (cw_ref[2:3, lo:hi] * gate
                  + cw_ref[1:2, lo:hi] * _shifted(prev, gate, 1)
                  + cw_ref[0:1, lo:hi] * _shifted(prev, gate, 2))
            act_s[:, lo:hi] = (cv * _sigmoid(cv) * up).astype(BF16)
        y = x + _dot(act_s[...], wd_s[...])
        if final_norm:
            y = _rms(y, gf_ref[...])
        o_ref[...] = y


def _conv_gated_mlp(x, g, w_up, cw, wd, g_final, *, layer, seq, final_norm):
    n, d = x.shape
    assert UP_SLAB % CHUNK == 0 and D_FF % UP_SLAB == CHUNK % UP_SLAB
    kern = functools.partial(_ffn_kernel, tiles_per_seq=seq // TM, final_norm=final_norm)
    last = FFN_PREP - 1
    tile = lambda i: (jnp.maximum(i - FFN_PREP, 0), 0)
    return pl.pallas_call(
        kern,
        out_shape=jax.ShapeDtypeStruct((n, d), F32),
        grid=(FFN_PREP + n // TM,),
        in_specs=[
            pl.BlockSpec((TM, d), tile),
            _layer(g.shape, layer),
            pl.BlockSpec((None, d, UP_SLAB), lambda i: (layer, 0, jnp.minimum(i, last))),
            _layer(cw.shape, layer),
            pl.BlockSpec((None, CHUNK, d), lambda i: (layer, jnp.minimum(i, last), 0)),
            _resident(g_final.shape),
        ],
        out_specs=pl.BlockSpec((TM, d), tile),
        scratch_shapes=[pltpu.VMEM((TM, D_FF), BF16),
                        pltpu.VMEM((SUBLANES, D_FF), F32),
                        pltpu.VMEM((FFN_PREP, d, UP_SLAB), BF16),
                        pltpu.VMEM((D_FF, d), BF16)],
        compiler_params=_params(),
        name="conv_gated_mlp",
    )(x, g, w_up, cw, wd, g_final)


def kernel(x, mem, norm_mix, norm_xattn, norm_mem, norm_ffn, norm_final, a_w_in, a_conv, a_w_out, b_w_qkv, b_b_qkv, b_sinks, b_w_o, b_b_o, c_w_pw1, c_b_pw1, c_w_dw, c_b_dw, c_ln_g, c_ln_b, c_w_pw2, c_b_pw2, x_w_q, x_w_kv, x_w_o, f_w_up, f_conv, f_w_down):
    bsz, seq, d = x.shape
    depth = norm_mix.shape[0]
    assert d == D_MODEL and seq % TM == 0 and TM % BLOCK == 0
    row = lambda p: p.reshape(p.shape[0], 1, p.shape[1])

    h = x.reshape(bsz * seq, d)
    m_all, vw_all = _memory_fold(mem, row(norm_mem), x_w_kv, x_w_q, x_w_o)
    g_mix, g_xattn, g_ffn = row(norm_mix), row(norm_xattn), row(norm_ffn)

    for i in range(depth):
        kind, j = i % 3, i // 3
        if kind == 0:
            h = _sconv_mixer(h, g_mix, a_w_in, a_conv, a_w_out, layer=i, j=j, seq=seq)
        elif kind == 1:
            h = _swa_mixer(h, g_mix, b_w_qkv, row(b_b_qkv), b_sinks[j], b_w_o, row(b_b_o),
                           layer=i, j=j, seq=seq)
        else:
            h = _conformer_mixer(h, g_mix, c_w_pw1, row(c_b_pw1), c_w_dw, row(c_b_dw),
                                 row(c_ln_g), row(c_ln_b), c_w_pw2, row(c_b_pw2),
                                 layer=i, j=j, seq=seq)
        h = _memory_xattn(h, g_xattn, m_all, vw_all, layer=i, seq=seq)
        h = _conv_gated_mlp(h, g_ffn, f_w_up, f_conv, f_w_down, norm_final.reshape(1, d),
                            layer=i, seq=seq, final_norm=(i == depth - 1))
    return h.reshape(bsz, seq, d)
```

```python
import functools

import jax
import jax.numpy as jnp
from jax import lax
from jax.experimental import pallas as pl
from jax.experimental.pallas import tpu as pltpu

EPS = 1e-6
F32 = jnp.float32
BF16 = jnp.bfloat16

D_MODEL = 1024
SC_WIDTH = 3
ATTN_HEADS = 16
ATTN_KV_HEADS = 4
HEAD_DIM = 64
WINDOW = 128
BLOCK = 128
CONF_WIDTH = 31
XATTN_HEADS = 4
XATTN_HEAD_DIM = D_MODEL // XATTN_HEADS
D_FF = 2816
FFN_CONV_WIDTH = 3

SUBLANES = 8
LANES = 128
MXU_DIM = 256
VMEM_LIMIT_BYTES = 56 * 1024 * 1024

TM = 512
CHUNK = MXU_DIM
CONF_CARRY = 32
CONF_STRIDE = 4


def _rms(x, g):
    ms = jnp.mean(x * x, axis=-1, keepdims=True)
    return x * lax.rsqrt(ms + EPS) * g


def _dot(a, b):
    return jnp.dot(a, b, preferred_element_type=F32)


def _dot_nt(a, b):
    return lax.dot_general(a, b, (((1,), (1,)), ((), ())), preferred_element_type=F32)


def _sigmoid(x):
    return 1.0 / (1.0 + jnp.exp(-x))


def _shifted(prev, cur, shift):
    n_prev = prev.shape[0]
    ext = jnp.concatenate([prev, cur], axis=0)
    return pltpu.roll(ext, shift, 0)[n_prev:]


def _to_bf16(src_ref, dst_ref, rows=256):
    for r in range(0, src_ref.shape[0], rows):
        dst_ref[r:r + rows, :] = src_ref[r:r + rows, :].astype(BF16)


def _resident(shape):
    nd = len(shape)
    return pl.BlockSpec(shape, lambda *_: (0,) * nd, pipeline_mode=pl.Buffered(1))


def _layer(shape, layer):
    nd = len(shape) - 1
    return pl.BlockSpec((None,) + tuple(shape[1:]), lambda *_: (layer,) + (0,) * nd,
                        pipeline_mode=pl.Buffered(1))


def _params():
    return pltpu.CompilerParams(dimension_semantics=("arbitrary",),
                                vmem_limit_bytes=VMEM_LIMIT_BYTES)


def _sconv_kernel(x_ref, g_ref, win_f32, cw_ref, wout_f32, o_ref, carry_ref, z_ref,
                  win_ref, wout_ref, *, tiles_per_seq):
    tm = x_ref.shape[0]

    @pl.when(pl.program_id(0) == 0)
    def _():
        _to_bf16(win_f32, win_ref)
        _to_bf16(wout_f32, wout_ref)

    first = (pl.program_id(0) % tiles_per_seq) == 0
    x = x_ref[...]
    xn = _rms(x, g_ref[...]).astype(BF16)
    cw = cw_ref[...]
    for j in range(D_MODEL // CHUNK):
        lo, hi = j * CHUNK, (j + 1) * CHUNK
        b = _dot(xn, win_ref[:, lo:hi])
        c = _dot(xn, win_ref[:, D_MODEL + lo:D_MODEL + hi])
        v = _dot(xn, win_ref[:, 2 * D_MODEL + lo:2 * D_MODEL + hi])
        u = c * v
        prev = jnp.where(first, 0.0, carry_ref[:, lo:hi])
        carry_ref[:, lo:hi] = u[tm - SUBLANES:, :]
        conv = (cw[2:3, lo:hi] * u
                + cw[1:2, lo:hi] * _shifted(prev, u, 1)
                + cw[0:1, lo:hi] * _shifted(prev, u, 2))
        z_ref[:, lo:hi] = (b * conv).astype(BF16)
    o_ref[...] = x + _dot(z_ref[...], wout_ref[...])


def _sconv_mixer(x, g, w_in, conv_w, w_out, *, layer, j, seq):
    n, d = x.shape
    kern = functools.partial(_sconv_kernel, tiles_per_seq=seq // TM)
    return pl.pallas_call(
        kern,
        out_shape=jax.ShapeDtypeStruct((n, d), F32),
        grid=(n // TM,),
        in_specs=[
            pl.BlockSpec((TM, d), lambda i: (i, 0)),
            _layer(g.shape, layer),
            _layer(w_in.shape, j),
            _layer(conv_w.shape, j),
            _layer(w_out.shape, j),
        ],
        out_specs=pl.BlockSpec((TM, d), lambda i: (i, 0)),
        scratch_shapes=[pltpu.VMEM((SUBLANES, d), F32),
                        pltpu.VMEM((TM, d), BF16),
                        pltpu.VMEM(w_in.shape[1:], BF16),
                        pltpu.VMEM(w_out.shape[1:], BF16)],
        compiler_params=_params(),
        name="sconv_mixer",
    )(x, g, w_in, conv_w, w_out)


def _swa_kernel(sinks_ref, x_ref, g_ref, wqkv_f32, bqkv_ref, wo_f32, bo_ref, o_ref,
                q_s, k2_s, v2_s, oh_s, bias_s, wqkv_ref, wo_ref, *, tiles_per_seq):
    tm = x_ref.shape[0]
    n_blocks = tm // BLOCK
    group = ATTN_HEADS // ATTN_KV_HEADS
    q_dim = ATTN_HEADS * HEAD_DIM
    kv_dim = ATTN_KV_HEADS * HEAD_DIM
    first = (pl.program_id(0) % tiles_per_seq) == 0

    @pl.when(first)
    def _():
        k2_s[:, :, 0:BLOCK, :] = jnp.zeros((ATTN_KV_HEADS, 2, BLOCK, LANES), BF16)
        v2_s[:, :, 0:BLOCK, :] = jnp.zeros((ATTN_KV_HEADS, 2, BLOCK, LANES), BF16)

    @pl.when(pl.program_id(0) == 0)
    def _():
        _to_bf16(wqkv_f32, wqkv_ref)
        _to_bf16(wo_f32, wo_ref)
        qi = lax.broadcasted_iota(jnp.int32, (BLOCK, 2 * BLOCK), 0)
        kj = lax.broadcasted_iota(jnp.int32, (BLOCK, 2 * BLOCK), 1)
        dist = BLOCK + qi - kj
        neg_inf = jnp.float32(-jnp.inf)
        pen = jnp.where((dist >= 0) & (dist < WINDOW), 0.0, neg_inf)
        pen_start = jnp.where(kj < BLOCK, neg_inf, pen)
        distf = dist.astype(F32)
        for h in range(ATTN_HEADS):
            slope = 2.0 ** (-8.0 * (h + 1) / ATTN_HEADS)
            bias_s[0, h] = pen - slope * distf
            bias_s[1, h] = pen_start - slope * distf

    x = x_ref[...]
    xn = _rms(x, g_ref[...]).astype(BF16)
    scale = HEAD_DIM ** -0.5
    for j in range(q_dim // CHUNK):
        lo, hi = j * CHUNK, (j + 1) * CHUNK
        q_s[:, lo:hi] = ((_dot(xn, wqkv_ref[:, lo:hi]) + bqkv_ref[:, lo:hi]) * scale).astype(BF16)

    lane = lax.broadcasted_iota(jnp.int32, (tm, LANES), 1)
    for (dst, base) in ((k2_s, q_dim), (v2_s, q_dim + kv_dim)):
        for gi in range(kv_dim // LANES):
            lo = base + gi * LANES
            kg = _dot(xn, wqkv_ref[:, lo:lo + LANES]) + bqkv_ref[:, lo:lo + LANES]
            low = jnp.where(lane < HEAD_DIM, kg, 0.0)
            high = jnp.where(lane >= HEAD_DIM, kg, 0.0)
            low_hi = pltpu.roll(low, HEAD_DIM, 1)
            high_lo = pltpu.roll(high, HEAD_DIM, 1)
            dst[2 * gi, 0, BLOCK:, :] = low.astype(BF16)
            dst[2 * gi, 1, BLOCK:, :] = low_hi.astype(BF16)
            dst[2 * gi + 1, 0, BLOCK:, :] = high_lo.astype(BF16)
            dst[2 * gi + 1, 1, BLOCK:, :] = high.astype(BF16)

    for n in range(n_blocks):
        row = n * BLOCK
        variant = jnp.where(first, 1, 0) if n == 0 else 0
        for kh in range(ATTN_KV_HEADS):
            heads = range(kh * group, (kh + 1) * group)
            scores = []
            for h in heads:
                qp = q_s[pl.ds(row, BLOCK), (h // 2) * LANES:(h // 2 + 1) * LANES]
                kb = k2_s[kh, h % 2, pl.ds(row, 2 * BLOCK), :]
                scores.append(_dot_nt(qp, kb))
            outs = []
            for h, s in zip(heads, scores):
                s = s + bias_s[variant, h]
                sink = sinks_ref[h]
                m = jnp.maximum(jnp.max(s, axis=-1, keepdims=True), sink)
                e = jnp.exp(s - m)
                denom = jnp.sum(e, axis=-1, keepdims=True) + jnp.exp(sink - m)
                vb = v2_s[kh, h % 2, pl.ds(row, 2 * BLOCK), :]
                outs.append(_dot(e.astype(BF16), vb) * (1.0 / denom))
            for i in range(group // 2):
                p = (kh * group) // 2 + i
                o_pair = outs[2 * i] + outs[2 * i + 1]
                oh_s[pl.ds(row, BLOCK), p * LANES:(p + 1) * LANES] = o_pair.astype(BF16)

    k2_s[:, :, 0:BLOCK, :] = k2_s[:, :, tm:tm + BLOCK, :]
    v2_s[:, :, 0:BLOCK, :] = v2_s[:, :, tm:tm + BLOCK, :]

    o_ref[...] = x + _dot(oh_s[...], wo_ref[...]) + bo_ref[...]


def _swa_mixer(x, g, w_qkv, b_qkv, sinks, w_o, b_o, *, layer, j, seq):
    n, d = x.shape
    kern = functools.partial(_swa_kernel, tiles_per_seq=seq // TM)
    q_dim = ATTN_HEADS * HEAD_DIM
    return pl.pallas_call(
        kern,
        out_shape=jax.ShapeDtypeStruct((n, d), F32),
        grid=(n // TM,),
        in_specs=[
            pl.BlockSpec(memory_space=pltpu.SMEM),
            pl.BlockSpec((TM, d), lambda i: (i, 0)),
            _layer(g.shape, layer),
            _layer(w_qkv.shape, j),
            _layer(b_qkv.shape, j),
            _layer(w_o.shape, j),
            _layer(b_o.shape, j),
        ],
        out_specs=pl.BlockSpec((TM, d), lambda i: (i, 0)),
        scratch_shapes=[
            pltpu.VMEM((TM, q_dim), BF16),
            pltpu.VMEM((ATTN_KV_HEADS, 2, BLOCK + TM, LANES), BF16),
            pltpu.VMEM((ATTN_KV_HEADS, 2, BLOCK + TM, LANES), BF16),
            pltpu.VMEM((TM, q_dim), BF16),
            pltpu.VMEM((2, ATTN_HEADS, BLOCK, 2 * BLOCK), F32),
            pltpu.VMEM(w_qkv.shape[1:], BF16),
            pltpu.VMEM(w_o.shape[1:], BF16),
        ],
        compiler_params=_params(),
        name="swa_mixer",
    )(sinks, x, g, w_qkv, b_qkv, w_o, b_o)


def _conformer_kernel(x_ref, g_ref, w1_f32, b1_ref, wdw_ref, bdw_ref, lng_ref, lnb_ref,
                      w2_f32, b2_ref, o_ref, ext_s, u_s, w1_ref, w2_ref, *, tiles_per_seq):
    tm = x_ref.shape[0]
    n_slabs = D_MODEL // LANES
    rows = tm // CONF_STRIDE

    @pl.when(pl.program_id(0) == 0)
    def _():
        _to_bf16(w1_f32, w1_ref)
        _to_bf16(w2_f32, w2_ref)

    first = (pl.program_id(0) % tiles_per_seq) == 0
    x = x_ref[...]
    xn = _rms(x, g_ref[...]).astype(BF16)
    for j in range(D_MODEL // CHUNK):
        lo, hi = j * CHUNK, (j + 1) * CHUNK
        a = _dot(xn, w1_ref[:, lo:hi]) + b1_ref[:, lo:hi]
        gate = _dot(xn, w1_ref[:, D_MODEL + lo:D_MODEL + hi]) + b1_ref[:, D_MODEL + lo:D_MODEL + hi]
        u = a * _sigmoid(gate)
        for i in range(CHUNK // LANES):
            k = j * (CHUNK // LANES) + i
            klo, khi = k * LANES, (k + 1) * LANES
            ext_s[k, 0:CONF_CARRY, :] = jnp.where(first, 0.0, ext_s[k, tm:tm + CONF_CARRY, :])
            ext_s[k, CONF_CARRY:CONF_CARRY + tm, :] = u[:, i * LANES:(i + 1) * LANES]
            for p in range(CONF_STRIDE):
                acc = jnp.zeros((rows, LANES), F32) + bdw_ref[:, klo:khi]
                for delay in range(CONF_WIDTH):
                    tap = CONF_WIDTH - 1 - delay
                    src = ext_s[k, pl.ds(CONF_CARRY + p - delay, rows, stride=CONF_STRIDE), :]
                    acc = acc + wdw_ref[tap:tap + 1, klo:khi] * src
                u_s[k, pl.ds(p, rows, stride=CONF_STRIDE), :] = acc
    u = jnp.concatenate([u_s[k] for k in range(n_slabs)], axis=1)
    mu = jnp.mean(u, axis=-1, keepdims=True)
    uc = u - mu
    y = uc * lax.rsqrt(jnp.mean(uc * uc, axis=-1, keepdims=True) + EPS)
    y = y * lng_ref[...] + lnb_ref[...]
    y = (y * _sigmoid(y)).astype(BF16)
    o_ref[...] = x + _dot(y, w2_ref[...]) + b2_ref[...]


def _conformer_mixer(x, g, w1, b1, wdw, bdw, lng, lnb, w2, b2, *, layer, j, seq):
    n, d = x.shape
    kern = functools.partial(_conformer_kernel, tiles_per_seq=seq // TM)
    return pl.pallas_call(
        kern,
        out_shape=jax.ShapeDtypeStruct((n, d), F32),
        grid=(n // TM,),
        in_specs=[pl.BlockSpec((TM, d), lambda i: (i, 0))]
        + [_layer(g.shape, layer)]
        + [_layer(a.shape, j) for a in (w1, b1, wdw, bdw, lng, lnb, w2, b2)],
        out_specs=pl.BlockSpec((TM, d), lambda i: (i, 0)),
        scratch_shapes=[pltpu.VMEM((d // LANES, CONF_CARRY + TM, LANES), F32),
                        pltpu.VMEM((d // LANES, TM, LANES), F32),
                        pltpu.VMEM(w1.shape[1:], BF16),
                        pltpu.VMEM(w2.shape[1:], BF16)],
        compiler_params=_params(),
        name="conformer_mixer",
    )(x, g, w1, b1, wdw, bdw, lng, lnb, w2, b2)


def _memfold_kernel(mem_ref, g_ref, wkv_ref, wq_ref, wo_ref, m_ref, vw_ref):
    d = mem_ref.shape[-1]
    hd = XATTN_HEAD_DIM
    mn = _rms(mem_ref[...], g_ref[...]).astype(BF16)
    k = _dot(mn, wkv_ref[:, :d].astype(BF16)).astype(BF16)
    v = _dot(mn, wkv_ref[:, d:].astype(BF16)).astype(BF16)
    scale = hd ** -0.5
    for h in range(XATTN_HEADS):
        lo, hi = h * hd, (h + 1) * hd
        m_ref[:, lo:hi] = (_dot_nt(wq_ref[:, lo:hi].astype(BF16), k[:, lo:hi]) * scale).astype(BF16)
        vw_ref[lo:hi, :] = _dot(v[:, lo:hi], wo_ref[lo:hi, :].astype(BF16)).astype(BF16)


def _memory_fold(mem, g_mem, w_kv, w_q, w_o):
    bsz, m, d = mem.shape
    depth = w_kv.shape[0]
    hm = XATTN_HEADS * m
    return pl.pallas_call(
        _memfold_kernel,
        out_shape=(jax.ShapeDtypeStruct((depth, bsz, d, hm), BF16),
                   jax.ShapeDtypeStruct((depth, bsz, hm, d), BF16)),
        grid=(depth, bsz),
        in_specs=[
            pl.BlockSpec((None, m, d), lambda l, b: (b, 0, 0)),
            pl.BlockSpec((None, 1, d), lambda l, b: (l, 0, 0)),
            pl.BlockSpec((None, d, 2 * d), lambda l, b: (l, 0, 0)),
            pl.BlockSpec((None, d, d), lambda l, b: (l, 0, 0)),
            pl.BlockSpec((None, d, d), lambda l, b: (l, 0, 0)),
        ],
        out_specs=(pl.BlockSpec((None, None, d, hm), lambda l, b: (l, b, 0, 0)),
                   pl.BlockSpec((None, None, hm, d), lambda l, b: (l, b, 0, 0))),
        compiler_params=pltpu.CompilerParams(
            dimension_semantics=("arbitrary", "arbitrary"),
            vmem_limit_bytes=VMEM_LIMIT_BYTES),
        name="memory_fold",
    )(mem, g_mem, w_kv, w_q, w_o)


def _xattn_tile(x, g_ref, m_ref, vw_ref, p_s):
    n_mem = m_ref.shape[-1] // XATTN_HEADS
    xn = _rms(x, g_ref[...]).astype(BF16)
    scores = [_dot(xn, m_ref[:, h * n_mem:(h + 1) * n_mem]) for h in range(XATTN_HEADS)]
    for h, s in enumerate(scores):
        e = jnp.exp(s - jnp.max(s, axis=-1, keepdims=True))
        inv = 1.0 / jnp.sum(e, axis=-1, keepdims=True)
        p_s[:, h * n_mem:(h + 1) * n_mem] = (e * inv).astype(BF16)
    return x + _dot(p_s[...], vw_ref[...])


FFN_PREP = D_FF // CHUNK
UP_SLAB = 2 * D_FF // FFN_PREP


def _up_cols(ref, col):
    slab, off = divmod(col, UP_SLAB)
    return ref[slab, :, off:off + CHUNK]


def _xattn_ffn_kernel(x_ref, gx_ref, m_ref, vw_ref, g_ref, wup_slab, cw_ref, wd_slab, gf_ref,
                      o_ref, p_s, act_s, carry_s, wup_s, wd_s, *, tiles_per_seq, final_norm):
    tm = x_ref.shape[0]
    step = pl.program_id(0)

    @pl.when(step < FFN_PREP)
    def _():
        wup_s[step] = wup_slab[...].astype(BF16)
        row = pl.multiple_of(step * CHUNK, CHUNK)
        wd_s[pl.ds(row, CHUNK), :] = wd_slab[...].astype(BF16)

    @pl.when(step >= FFN_PREP)
    def _():
        first = ((step - FFN_PREP) % tiles_per_seq) == 0
        x = _xattn_tile(x_ref[...], gx_ref, m_ref, vw_ref, p_s)
        xn = _rms(x, g_ref[...]).astype(BF16)
        for j in range(D_FF // CHUNK):
            lo, hi = j * CHUNK, (j + 1) * CHUNK
            gate = _dot(xn, _up_cols(wup_s, lo))
            up = _dot(xn, _up_cols(wup_s, D_FF + lo))
            prev = jnp.where(first, 0.0, carry_s[:, lo:hi])
            carry_s[:, lo:hi] = gate[tm - SUBLANES:, :]
            cv = (cw_ref[2:3, lo:hi] * gate
                  + cw_ref[1:2, lo:hi] * _shifted(prev, gate, 1)
                  + cw_ref[0:1, lo:hi] * _shifted(prev, gate, 2))
            act_s[:, lo:hi] = (cv * _sigmoid(cv) * up).astype(BF16)
        y = x + _dot(act_s[...], wd_s[...])
        if final_norm:
            y = _rms(y, gf_ref[...])
        o_ref[...] = y


def _xattn_mlp(x, gx, m_all, vw_all, g, w_up, cw, wd, g_final, *, layer, seq, final_norm):
    n, d = x.shape
    hm = m_all.shape[-1]
    tiles_per_seq = seq // TM
    assert UP_SLAB % CHUNK == 0 and D_FF % UP_SLAB == CHUNK % UP_SLAB
    kern = functools.partial(_xattn_ffn_kernel, tiles_per_seq=tiles_per_seq, final_norm=final_norm)
    last = FFN_PREP - 1
    tile = lambda i: (jnp.maximum(i - FFN_PREP, 0), 0)
    batch = lambda i: (layer, jnp.maximum(i - FFN_PREP, 0) // tiles_per_seq, 0, 0)
    return pl.pallas_call(
        kern,
        out_shape=jax.ShapeDtypeStruct((n, d), F32),
        grid=(FFN_PREP + n // TM,),
        in_specs=[
            pl.BlockSpec((TM, d), tile),
            _layer(gx.shape, layer),
            pl.BlockSpec((None, None, d, hm), batch),
            pl.BlockSpec((None, None, hm, d), batch),
            _layer(g.shape, layer),
            pl.BlockSpec((None, d, UP_SLAB), lambda i: (layer, 0, jnp.minimum(i, last))),
            _layer(cw.shape, layer),
            pl.BlockSpec((None, CHUNK, d), lambda i: (layer, jnp.minimum(i, last), 0)),
            _resident(g_final.shape),
        ],
        out_specs=pl.BlockSpec((TM, d), tile),
        scratch_shapes=[pltpu.VMEM((TM, hm), BF16),
                        pltpu.VMEM((TM, D_FF), BF16),
                        pltpu.VMEM((SUBLANES, D_FF), F32),
                        pltpu.VMEM((FFN_PREP, d, UP_SLAB), BF16),
                        pltpu.VMEM((D_FF, d), BF16)],
        compiler_params=_params(),
        name="xattn_mlp",
    )(x, gx, m_all, vw_all, g, w_up, cw, wd, g_final)


def kernel(x, mem, norm_mix, norm_xattn, norm_mem, norm_ffn, norm_final, a_w_in, a_conv, a_w_out, b_w_qkv, b_b_qkv, b_sinks, b_w_o, b_b_o, c_w_pw1, c_b_pw1, c_w_dw, c_b_dw, c_ln_g, c_ln_b, c_w_pw2, c_b_pw2, x_w_q, x_w_kv, x_w_o, f_w_up, f_conv, f_w_down):
    bsz, seq, d = x.shape
    depth = norm_mix.shape[0]
    assert d == D_MODEL and seq % TM == 0 and TM % BLOCK == 0
    row = lambda p: p.reshape(p.shape[0], 1, p.shape[1])

    h = x.reshape(bsz * seq, d)
    m_all, vw_all = _memory_fold(mem, row(norm_mem), x_w_kv, x_w_q, x_w_o)
    g_mix, g_xattn, g_ffn = row(norm_mix), row(norm_xattn), row(norm_ffn)

    for i in range(depth):
        kind, j = i % 3, i // 3
        if kind == 0:
            h = _sconv_mixer(h, g_mix, a_w_in, a_conv, a_w_out, layer=i, j=j, seq=seq)
        elif kind == 1:
            h = _swa_mixer(h, g_mix, b_w_qkv, row(b_b_qkv), b_sinks[j], b_w_o, row(b_b_o),
                           layer=i, j=j, seq=seq)
        else:
            h = _conformer_mixer(h, g_mix, c_w_pw1, row(c_b_pw1), c_w_dw, row(c_b_dw),
                                 row(c_ln_g), row(c_ln_b), c_w_pw2, row(c_b_pw2),
                                 layer=i, j=j, seq=seq)
        h = _xattn_mlp(h, g_xattn, m_all, vw_all, g_ffn, f_w_up, f_conv, f_w_down,
                       norm_final.reshape(1, d), layer=i, seq=seq, final_norm=(i == depth - 1))
    return h.reshape(bsz, seq, d)
```

```python
import functools

import jax
import jax.numpy as jnp
from jax import lax
from jax.experimental import pallas as pl
from jax.experimental.pallas import tpu as pltpu

EPS = 1e-6
F32 = jnp.float32
BF16 = jnp.bfloat16

D_MODEL = 1024
SC_WIDTH = 3
ATTN_HEADS = 16
ATTN_KV_HEADS = 4
HEAD_DIM = 64
WINDOW = 128
BLOCK = 128
CONF_WIDTH = 31
XATTN_HEADS = 4
XATTN_HEAD_DIM = D_MODEL // XATTN_HEADS
D_FF = 2816
FFN_CONV_WIDTH = 3

SUBLANES = 8
LANES = 128
MXU_DIM = 256
VMEM_LIMIT_BYTES = 56 * 1024 * 1024

TM = 512
CHUNK = MXU_DIM
CONF_CARRY = 32
CONF_STRIDE = 4


def _rms(x, g):
    ms = jnp.mean(x * x, axis=-1, keepdims=True)
    return x * lax.rsqrt(ms + EPS) * g


def _rms_split(x, g):
    r = lax.rsqrt(jnp.mean(x * x, axis=-1, keepdims=True) + EPS)
    return (x * g).astype(BF16), r


def _dot(a, b):
    return jnp.dot(a, b, preferred_element_type=F32)


def _dot_nt(a, b):
    return lax.dot_general(a, b, (((1,), (1,)), ((), ())), preferred_element_type=F32)


def _sigmoid(x):
    return 1.0 / (1.0 + jnp.exp(-x))


def _shifted(prev, cur, shift):
    n_prev = prev.shape[0]
    ext = jnp.concatenate([prev, cur], axis=0)
    return pltpu.roll(ext, shift, 0)[n_prev:]


def _to_bf16(src_ref, dst_ref, rows=256):
    for r in range(0, src_ref.shape[0], rows):
        dst_ref[r:r + rows, :] = src_ref[r:r + rows, :].astype(BF16)


def _resident(shape):
    nd = len(shape)
    return pl.BlockSpec(shape, lambda *_: (0,) * nd, pipeline_mode=pl.Buffered(1))


def _layer(shape, layer):
    nd = len(shape) - 1
    return pl.BlockSpec((None,) + tuple(shape[1:]), lambda *_: (layer,) + (0,) * nd,
                        pipeline_mode=pl.Buffered(1))


def _params():
    return pltpu.CompilerParams(dimension_semantics=("arbitrary",),
                                vmem_limit_bytes=VMEM_LIMIT_BYTES)


def _sconv_kernel(x_ref, g_ref, win_f32, cw_ref, wout_f32, o_ref, carry_ref, z_ref,
                  win_ref, wout_ref, *, tiles_per_seq):
    tm = x_ref.shape[0]

    @pl.when(pl.program_id(0) == 0)
    def _():
        _to_bf16(win_f32, win_ref)
        _to_bf16(wout_f32, wout_ref)

    first = (pl.program_id(0) % tiles_per_seq) == 0
    x = x_ref[...]
    xg, r = _rms_split(x, g_ref[...])
    cw = cw_ref[...]
    for j in range(D_MODEL // CHUNK):
        lo, hi = j * CHUNK, (j + 1) * CHUNK
        b = _dot(xg, win_ref[:, lo:hi]) * r
        c = _dot(xg, win_ref[:, D_MODEL + lo:D_MODEL + hi]) * r
        v = _dot(xg, win_ref[:, 2 * D_MODEL + lo:2 * D_MODEL + hi]) * r
        u = c * v
        prev = jnp.where(first, 0.0, carry_ref[:, lo:hi])
        carry_ref[:, lo:hi] = u[tm - SUBLANES:, :]
        conv = (cw[2:3, lo:hi] * u
                + cw[1:2, lo:hi] * _shifted(prev, u, 1)
                + cw[0:1, lo:hi] * _shifted(prev, u, 2))
        z_ref[:, lo:hi] = (b * conv).astype(BF16)
    o_ref[...] = x + _dot(z_ref[...], wout_ref[...])


def _sconv_mixer(x, g, w_in, conv_w, w_out, *, layer, j, seq):
    n, d = x.shape
    kern = functools.partial(_sconv_kernel, tiles_per_seq=seq // TM)
    return pl.pallas_call(
        kern,
        out_shape=jax.ShapeDtypeStruct((n, d), F32),
        grid=(n // TM,),
        in_specs=[
            pl.BlockSpec((TM, d), lambda i: (i, 0)),
            _layer(g.shape, layer),
            _layer(w_in.shape, j),
            _layer(conv_w.shape, j),
            _layer(w_out.shape, j),
        ],
        out_specs=pl.BlockSpec((TM, d), lambda i: (i, 0)),
        scratch_shapes=[pltpu.VMEM((SUBLANES, d), F32),
                        pltpu.VMEM((TM, d), BF16),
                        pltpu.VMEM(w_in.shape[1:], BF16),
                        pltpu.VMEM(w_out.shape[1:], BF16)],
        compiler_params=_params(),
        name="sconv_mixer",
    )(x, g, w_in, conv_w, w_out)


def _swa_kernel(sinks_ref, x_ref, g_ref, wqkv_f32, bqkv_ref, wo_f32, bo_ref, o_ref,
                q_s, k2_s, v2_s, oh_s, bias_s, wqkv_ref, wo_ref, *, tiles_per_seq):
    tm = x_ref.shape[0]
    n_blocks = tm // BLOCK
    group = ATTN_HEADS // ATTN_KV_HEADS
    q_dim = ATTN_HEADS * HEAD_DIM
    kv_dim = ATTN_KV_HEADS * HEAD_DIM
    first = (pl.program_id(0) % tiles_per_seq) == 0

    @pl.when(first)
    def _():
        k2_s[:, :, 0:BLOCK, :] = jnp.zeros((ATTN_KV_HEADS, 2, BLOCK, LANES), BF16)
        v2_s[:, :, 0:BLOCK, :] = jnp.zeros((ATTN_KV_HEADS, 2, BLOCK, LANES), BF16)

    @pl.when(pl.program_id(0) == 0)
    def _():
        _to_bf16(wqkv_f32, wqkv_ref)
        _to_bf16(wo_f32, wo_ref)
        qi = lax.broadcasted_iota(jnp.int32, (BLOCK, 2 * BLOCK), 0)
        kj = lax.broadcasted_iota(jnp.int32, (BLOCK, 2 * BLOCK), 1)
        dist = BLOCK + qi - kj
        neg_inf = jnp.float32(-jnp.inf)
        pen = jnp.where((dist >= 0) & (dist < WINDOW), 0.0, neg_inf)
        pen_start = jnp.where(kj < BLOCK, neg_inf, pen)
        distf = dist.astype(F32)
        for h in range(ATTN_HEADS):
            slope = 2.0 ** (-8.0 * (h + 1) / ATTN_HEADS)
            bias_s[0, h] = pen - slope * distf
            bias_s[1, h] = pen_start - slope * distf

    x = x_ref[...]
    xg, r = _rms_split(x, g_ref[...])
    scale = HEAD_DIM ** -0.5
    for j in range(q_dim // CHUNK):
        lo, hi = j * CHUNK, (j + 1) * CHUNK
        q_s[:, lo:hi] = ((_dot(xg, wqkv_ref[:, lo:hi]) * r + bqkv_ref[:, lo:hi]) * scale).astype(BF16)

    lane = lax.broadcasted_iota(jnp.int32, (tm, LANES), 1)
    for (dst, base) in ((k2_s, q_dim), (v2_s, q_dim + kv_dim)):
        for gi in range(kv_dim // LANES):
            lo = base + gi * LANES
            kg = _dot(xg, wqkv_ref[:, lo:lo + LANES]) * r + bqkv_ref[:, lo:lo + LANES]
            low = jnp.where(lane < HEAD_DIM, kg, 0.0)
            high = jnp.where(lane >= HEAD_DIM, kg, 0.0)
            low_hi = pltpu.roll(low, HEAD_DIM, 1)
            high_lo = pltpu.roll(high, HEAD_DIM, 1)
            dst[2 * gi, 0, BLOCK:, :] = low.astype(BF16)
            dst[2 * gi, 1, BLOCK:, :] = low_hi.astype(BF16)
            dst[2 * gi + 1, 0, BLOCK:, :] = high_lo.astype(BF16)
            dst[2 * gi + 1, 1, BLOCK:, :] = high.astype(BF16)

    for n in range(n_blocks):
        row = n * BLOCK
        variant = jnp.where(first, 1, 0) if n == 0 else 0
        for kh in range(ATTN_KV_HEADS):
            heads = range(kh * group, (kh + 1) * group)
            scores = []
            for h in heads:
                qp = q_s[pl.ds(row, BLOCK), (h // 2) * LANES:(h // 2 + 1) * LANES]
                kb = k2_s[kh, h % 2, pl.ds(row, 2 * BLOCK), :]
                scores.append(_dot_nt(qp, kb))
            outs = []
            for h, s in zip(heads, scores):
                s = s + bias_s[variant, h]
                sink = sinks_ref[h]
                m = jnp.maximum(jnp.max(s, axis=-1, keepdims=True), sink)
                e = jnp.exp(s - m)
                denom = jnp.sum(e, axis=-1, keepdims=True) + jnp.exp(sink - m)
                vb = v2_s[kh, h % 2, pl.ds(row, 2 * BLOCK), :]
                outs.append(_dot(e.astype(BF16), vb) * (1.0 / denom))
            for i in range(group // 2):
                p = (kh * group) // 2 + i
                o_pair = outs[2 * i] + outs[2 * i + 1]
                oh_s[pl.ds(row, BLOCK), p * LANES:(p + 1) * LANES] = o_pair.astype(BF16)

    k2_s[:, :, 0:BLOCK, :] = k2_s[:, :, tm:tm + BLOCK, :]
    v2_s[:, :, 0:BLOCK, :] = v2_s[:, :, tm:tm + BLOCK, :]

    o_ref[...] = x + _dot(oh_s[...], wo_ref[...]) + bo_ref[...]


def _swa_mixer(x, g, w_qkv, b_qkv, sinks, w_o, b_o, *, layer, j, seq):
    n, d = x.shape
    kern = functools.partial(_swa_kernel, tiles_per_seq=seq // TM)
    q_dim = ATTN_HEADS * HEAD_DIM
    return pl.pallas_call(
        kern,
        out_shape=jax.ShapeDtypeStruct((n, d), F32),
        grid=(n // TM,),
        in_specs=[
            pl.BlockSpec(memory_space=pltpu.SMEM),
            pl.BlockSpec((TM, d), lambda i: (i, 0)),
            _layer(g.shape, layer),
            _layer(w_qkv.shape, j),
            _layer(b_qkv.shape, j),
            _layer(w_o.shape, j),
            _layer(b_o.shape, j),
        ],
        out_specs=pl.BlockSpec((TM, d), lambda i: (i, 0)),
        scratch_shapes=[
            pltpu.VMEM((TM, q_dim), BF16),
            pltpu.VMEM((ATTN_KV_HEADS, 2, BLOCK + TM, LANES), BF16),
            pltpu.VMEM((ATTN_KV_HEADS, 2, BLOCK + TM, LANES), BF16),
            pltpu.VMEM((TM, q_dim), BF16),
            pltpu.VMEM((2, ATTN_HEADS, BLOCK, 2 * BLOCK), F32),
            pltpu.VMEM(w_qkv.shape[1:], BF16),
            pltpu.VMEM(w_o.shape[1:], BF16),
        ],
        compiler_params=_params(),
        name="swa_mixer",
    )(sinks, x, g, w_qkv, b_qkv, w_o, b_o)


def _conformer_kernel(x_ref, g_ref, w1_f32, b1_ref, wdw_ref, bdw_ref, lng_ref, lnb_ref,
                      w2_f32, b2_ref, o_ref, ext_s, u_s, w1_ref, w2_ref, *, tiles_per_seq):
    tm = x_ref.shape[0]
    n_slabs = D_MODEL // LANES
    rows = tm // CONF_STRIDE

    @pl.when(pl.program_id(0) == 0)
    def _():
        _to_bf16(w1_f32, w1_ref)
        _to_bf16(w2_f32, w2_ref)

    first = (pl.program_id(0) % tiles_per_seq) == 0
    x = x_ref[...]
    xn = _rms(x, g_ref[...]).astype(BF16)
    for j in range(D_MODEL // CHUNK):
        lo, hi = j * CHUNK, (j + 1) * CHUNK
        a = _dot(xn, w1_ref[:, lo:hi]) + b1_ref[:, lo:hi]
        gate = _dot(xn, w1_ref[:, D_MODEL + lo:D_MODEL + hi]) + b1_ref[:, D_MODEL + lo:D_MODEL + hi]
        u = a * _sigmoid(gate)
        for i in range(CHUNK // LANES):
            k = j * (CHUNK // LANES) + i
            klo, khi = k * LANES, (k + 1) * LANES
            ext_s[k, 0:CONF_CARRY, :] = jnp.where(first, 0.0, ext_s[k, tm:tm + CONF_CARRY, :])
            ext_s[k, CONF_CARRY:CONF_CARRY + tm, :] = u[:, i * LANES:(i + 1) * LANES]
            for p in range(CONF_STRIDE):
                acc = jnp.zeros((rows, LANES), F32) + bdw_ref[:, klo:khi]
                for delay in range(CONF_WIDTH):
                    tap = CONF_WIDTH - 1 - delay
                    src = ext_s[k, pl.ds(CONF_CARRY + p - delay, rows, stride=CONF_STRIDE), :]
                    acc = acc + wdw_ref[tap:tap + 1, klo:khi] * src
                u_s[k, pl.ds(p, rows, stride=CONF_STRIDE), :] = acc
    u = jnp.concatenate([u_s[k] for k in range(n_slabs)], axis=1)
    mu = jnp.mean(u, axis=-1, keepdims=True)
    uc = u - mu
    y = uc * lax.rsqrt(jnp.mean(uc * uc, axis=-1, keepdims=True) + EPS)
    y = y * lng_ref[...] + lnb_ref[...]
    y = (y * _sigmoid(y)).astype(BF16)
    o_ref[...] = x + _dot(y, w2_ref[...]) + b2_ref[...]


def _conformer_mixer(x, g, w1, b1, wdw, bdw, lng, lnb, w2, b2, *, layer, j, seq):
    n, d = x.shape
    kern = functools.partial(_conformer_kernel, tiles_per_seq=seq // TM)
    return pl.pallas_call(
        kern,
        out_shape=jax.ShapeDtypeStruct((n, d), F32),
        grid=(n // TM,),
        in_specs=[pl.BlockSpec((TM, d), lambda i: (i, 0))]
        + [_layer(g.shape, layer)]
        + [_layer(a.shape, j) for a in (w1, b1, wdw, bdw, lng, lnb, w2, b2)],
        out_specs=pl.BlockSpec((TM, d), lambda i: (i, 0)),
        scratch_shapes=[pltpu.VMEM((d // LANES, CONF_CARRY + TM, LANES), F32),
                        pltpu.VMEM((d // LANES, TM, LANES), F32),
                        pltpu.VMEM(w1.shape[1:], BF16),
                        pltpu.VMEM(w2.shape[1:], BF16)],
        compiler_params=_params(),
        name="conformer_mixer",
    )(x, g, w1, b1, wdw, bdw, lng, lnb, w2, b2)


def _memfold_kernel(mem_ref, g_ref, wkv_ref, wq_ref, wo_ref, m_ref, vw_ref):
    d = mem_ref.shape[-1]
    hd = XATTN_HEAD_DIM
    mn = _rms(mem_ref[...], g_ref[...]).astype(BF16)
    k = _dot(mn, wkv_ref[:, :d].astype(BF16)).astype(BF16)
    v = _dot(mn, wkv_ref[:, d:].astype(BF16)).astype(BF16)
    scale = hd ** -0.5
    for h in range(XATTN_HEADS):
        lo, hi = h * hd, (h + 1) * hd
        m_ref[:, lo:hi] = (_dot_nt(wq_ref[:, lo:hi].astype(BF16), k[:, lo:hi]) * scale).astype(BF16)
        vw_ref[lo:hi, :] = _dot(v[:, lo:hi], wo_ref[lo:hi, :].astype(BF16)).astype(BF16)


def _memory_fold(mem, g_mem, w_kv, w_q, w_o):
    bsz, m, d = mem.shape
    depth = w_kv.shape[0]
    hm = XATTN_HEADS * m
    return pl.pallas_call(
        _memfold_kernel,
        out_shape=(jax.ShapeDtypeStruct((depth, bsz, d, hm), BF16),
                   jax.ShapeDtypeStruct((depth, bsz, hm, d), BF16)),
        grid=(depth, bsz),
        in_specs=[
            pl.BlockSpec((None, m, d), lambda l, b: (b, 0, 0)),
            pl.BlockSpec((None, 1, d), lambda l, b: (l, 0, 0)),
            pl.BlockSpec((None, d, 2 * d), lambda l, b: (l, 0, 0)),
            pl.BlockSpec((None, d, d), lambda l, b: (l, 0, 0)),
            pl.BlockSpec((None, d, d), lambda l, b: (l, 0, 0)),
        ],
        out_specs=(pl.BlockSpec((None, None, d, hm), lambda l, b: (l, b, 0, 0)),
                   pl.BlockSpec((None, None, hm, d), lambda l, b: (l, b, 0, 0))),
        compiler_params=pltpu.CompilerParams(
            dimension_semantics=("arbitrary", "arbitrary"),
            vmem_limit_bytes=VMEM_LIMIT_BYTES),
        name="memory_fold",
    )(mem, g_mem, w_kv, w_q, w_o)


def _xattn_tile(x, g_ref, m_ref, vw_ref, p_s):
    n_mem = m_ref.shape[-1] // XATTN_HEADS
    xg, r = _rms_split(x, g_ref[...])
    scores = [_dot(xg, m_ref[:, h * n_mem:(h + 1) * n_mem]) for h in range(XATTN_HEADS)]
    for h, s in enumerate(scores):
        s = s * r
        e = jnp.exp(s - jnp.max(s, axis=-1, keepdims=True))
        inv = 1.0 / jnp.sum(e, axis=-1, keepdims=True)
        p_s[:, h * n_mem:(h + 1) * n_mem] = (e * inv).astype(BF16)
    return x + _dot(p_s[...], vw_ref[...])


FFN_PREP = D_FF // CHUNK
UP_SLAB = 2 * D_FF // FFN_PREP


def _up_cols(ref, col):
    slab, off = divmod(col, UP_SLAB)
    return ref[slab, :, off:off + CHUNK]


def _xattn_ffn_kernel(x_ref, gx_ref, m_ref, vw_ref, g_ref, wup_slab, cw_ref, wd_slab, gf_ref,
                      o_ref, p_s, act_s, carry_s, wup_s, wd_s, *, tiles_per_seq, final_norm):
    tm = x_ref.shape[0]
    step = pl.program_id(0)

    @pl.when(step < FFN_PREP)
    def _():
        wup_s[step] = wup_slab[...].astype(BF16)
        row = pl.multiple_of(step * CHUNK, CHUNK)
        wd_s[pl.ds(row, CHUNK), :] = wd_slab[...].astype(BF16)

    @pl.when(step >= FFN_PREP)
    def _():
        first = ((step - FFN_PREP) % tiles_per_seq) == 0
        x = _xattn_tile(x_ref[...], gx_ref, m_ref, vw_ref, p_s)
        xg, r = _rms_split(x, g_ref[...])
        for j in range(D_FF // CHUNK):
            lo, hi = j * CHUNK, (j + 1) * CHUNK
            gate = _dot(xg, _up_cols(wup_s, lo)) * r
            up = _dot(xg, _up_cols(wup_s, D_FF + lo)) * r
            prev = jnp.where(first, 0.0, carry_s[:, lo:hi])
            carry_s[:, lo:hi] = gate[tm - SUBLANES:, :]
            cv = (cw_ref[2:3, lo:hi] * gate
                  + cw_ref[1:2, lo:hi] * _shifted(prev, gate, 1)
                  + cw_ref[0:1, lo:hi] * _shifted(prev, gate, 2))
            act_s[:, lo:hi] = (cv * _sigmoid(cv) * up).astype(BF16)
        y = x + _dot(act_s[...], wd_s[...])
        if final_norm:
            y = _rms(y, gf_ref[...])
        o_ref[...] = y


def _xattn_mlp(x, gx, m_all, vw_all, g, w_up, cw, wd, g_final, *, layer, seq, final_norm):
    n, d = x.shape
    hm = m_all.shape[-1]
    tiles_per_seq = seq // TM
    assert UP_SLAB % CHUNK == 0 and D_FF % UP_SLAB == CHUNK % UP_SLAB
    kern = functools.partial(_xattn_ffn_kernel, tiles_per_seq=tiles_per_seq, final_norm=final_norm)
    last = FFN_PREP - 1
    tile = lambda i: (jnp.maximum(i - FFN_PREP, 0), 0)
    batch = lambda i: (layer, jnp.maximum(i - FFN_PREP, 0) // tiles_per_seq, 0, 0)
    return pl.pallas_call(
        kern,
        out_shape=jax.ShapeDtypeStruct((n, d), F32),
        grid=(FFN_PREP + n // TM,),
        in_specs=[
            pl.BlockSpec((TM, d), tile),
            _layer(gx.shape, layer),
            pl.BlockSpec((None, None, d, hm), batch),
            pl.BlockSpec((None, None, hm, d), batch),
            _layer(g.shape, layer),
            pl.BlockSpec((None, d, UP_SLAB), lambda i: (layer, 0, jnp.minimum(i, last))),
            _layer(cw.shape, layer),
            pl.BlockSpec((None, CHUNK, d), lambda i: (layer, jnp.minimum(i, last), 0)),
            _resident(g_final.shape),
        ],
        out_specs=pl.BlockSpec((TM, d), tile),
        scratch_shapes=[pltpu.VMEM((TM, hm), BF16),
                        pltpu.VMEM((TM, D_FF), BF16),
                        pltpu.VMEM((SUBLANES, D_FF), F32),
                        pltpu.VMEM((FFN_PREP, d, UP_SLAB), BF16),
                        pltpu.VMEM((D_FF, d), BF16)],
        compiler_params=_params(),
        name="xattn_mlp",
    )(x, gx, m_all, vw_all, g, w_up, cw, wd, g_final)


def kernel(x, mem, norm_mix, norm_xattn, norm_mem, norm_ffn, norm_final, a_w_in, a_conv, a_w_out, b_w_qkv, b_b_qkv, b_sinks, b_w_o, b_b_o, c_w_pw1, c_b_pw1, c_w_dw, c_b_dw, c_ln_g, c_ln_b, c_w_pw2, c_b_pw2, x_w_q, x_w_kv, x_w_o, f_w_up, f_conv, f_w_down):
    bsz, seq, d = x.shape
    depth = norm_mix.shape[0]
    assert d == D_MODEL and seq % TM == 0 and TM % BLOCK == 0
    row = lambda p: p.reshape(p.shape[0], 1, p.shape[1])

    h = x.reshape(bsz * seq, d)
    m_all, vw_all = _memory_fold(mem, row(norm_mem), x_w_kv, x_w_q, x_w_o)
    g_mix, g_xattn, g_ffn = row(norm_mix), row(norm_xattn), row(norm_ffn)

    for i in range(depth):
        kind, j = i % 3, i // 3
        if kind == 0:
            h = _sconv_mixer(h, g_mix, a_w_in, a_conv, a_w_out, layer=i, j=j, seq=seq)
        elif kind == 1:
            h = _swa_mixer(h, g_mix, b_w_qkv, row(b_b_qkv), b_sinks[j], b_w_o, row(b_b_o),
                           layer=i, j=j, seq=seq)
        else:
            h = _conformer_mixer(h, g_mix, c_w_pw1, row(c_b_pw1), c_w_dw, row(c_b_dw),
                                 row(c_ln_g), row(c_ln_b), c_w_pw2, row(c_b_pw2),
                                 layer=i, j=j, seq=seq)
        h = _xattn_mlp(h, g_xattn, m_all, vw_all, g_ffn, f_w_up, f_conv, f_w_down,
                       norm_final.reshape(1, d), layer=i, seq=seq, final_norm=(i == depth - 1))
    return h.reshape(bsz, seq, d)
```

```python
import functools

import jax
import jax.numpy as jnp
from jax import lax
from jax.experimental import pallas as pl
from jax.experimental.pallas import tpu as pltpu

EPS = 1e-6
F32 = jnp.float32
BF16 = jnp.bfloat16

D_MODEL = 1024
SC_WIDTH = 3
ATTN_HEADS = 16
ATTN_KV_HEADS = 4
HEAD_DIM = 64
WINDOW = 128
BLOCK = 128
CONF_WIDTH = 31
XATTN_HEADS = 4
XATTN_HEAD_DIM = D_MODEL // XATTN_HEADS
D_FF = 2816
FFN_CONV_WIDTH = 3

SUBLANES = 8
LANES = 128
MXU_DIM = 256
VMEM_LIMIT_BYTES = 56 * 1024 * 1024

TM = 512
CHUNK = MXU_DIM
CONF_CARRY = 32
CONF_STRIDE = 4
CONF_PARTS = 2


def _rms(x, g):
    ms = jnp.mean(x * x, axis=-1, keepdims=True)
    return x * lax.rsqrt(ms + EPS) * g


def _rms_split(x, g):
    r = lax.rsqrt(jnp.mean(x * x, axis=-1, keepdims=True) + EPS)
    return (x * g).astype(BF16), r


def _dot(a, b):
    return jnp.dot(a, b, preferred_element_type=F32)


def _dot_nt(a, b):
    return lax.dot_general(a, b, (((1,), (1,)), ((), ())), preferred_element_type=F32)


def _sigmoid(x):
    return 1.0 / (1.0 + jnp.exp(-x))


def _shifted(prev, cur, shift):
    n_prev = prev.shape[0]
    ext = jnp.concatenate([prev, cur], axis=0)
    return pltpu.roll(ext, shift, 0)[n_prev:]


def _to_bf16(src_ref, dst_ref, rows=256):
    for r in range(0, src_ref.shape[0], rows):
        dst_ref[r:r + rows, :] = src_ref[r:r + rows, :].astype(BF16)


def _resident(shape):
    nd = len(shape)
    return pl.BlockSpec(shape, lambda *_: (0,) * nd, pipeline_mode=pl.Buffered(1))


def _layer(shape, layer):
    nd = len(shape) - 1
    return pl.BlockSpec((None,) + tuple(shape[1:]), lambda *_: (layer,) + (0,) * nd,
                        pipeline_mode=pl.Buffered(1))


def _params():
    return pltpu.CompilerParams(dimension_semantics=("arbitrary",),
                                vmem_limit_bytes=VMEM_LIMIT_BYTES)


def _sconv_kernel(x_ref, g_ref, win_f32, cw_ref, wout_f32, o_ref, carry_ref, z_ref,
                  win_ref, wout_ref, *, tiles_per_seq):
    tm = x_ref.shape[0]

    @pl.when(pl.program_id(0) == 0)
    def _():
        _to_bf16(win_f32, win_ref)
        _to_bf16(wout_f32, wout_ref)

    first = (pl.program_id(0) % tiles_per_seq) == 0
    x = x_ref[...]
    xg, r = _rms_split(x, g_ref[...])
    cw = cw_ref[...]
    for j in range(D_MODEL // CHUNK):
        lo, hi = j * CHUNK, (j + 1) * CHUNK
        b = _dot(xg, win_ref[:, lo:hi]) * r
        c = _dot(xg, win_ref[:, D_MODEL + lo:D_MODEL + hi]) * r
        v = _dot(xg, win_ref[:, 2 * D_MODEL + lo:2 * D_MODEL + hi]) * r
        u = c * v
        prev = jnp.where(first, 0.0, carry_ref[:, lo:hi])
        carry_ref[:, lo:hi] = u[tm - SUBLANES:, :]
        conv = (cw[2:3, lo:hi] * u
                + cw[1:2, lo:hi] * _shifted(prev, u, 1)
                + cw[0:1, lo:hi] * _shifted(prev, u, 2))
        z_ref[:, lo:hi] = (b * conv).astype(BF16)
    o_ref[...] = x + _dot(z_ref[...], wout_ref[...])


def _sconv_mixer(x, g, w_in, conv_w, w_out, *, layer, j, seq):
    n, d = x.shape
    kern = functools.partial(_sconv_kernel, tiles_per_seq=seq // TM)
    return pl.pallas_call(
        kern,
        out_shape=jax.ShapeDtypeStruct((n, d), F32),
        grid=(n // TM,),
        in_specs=[
            pl.BlockSpec((TM, d), lambda i: (i, 0)),
            _layer(g.shape, layer),
            _layer(w_in.shape, j),
            _layer(conv_w.shape, j),
            _layer(w_out.shape, j),
        ],
        out_specs=pl.BlockSpec((TM, d), lambda i: (i, 0)),
        scratch_shapes=[pltpu.VMEM((SUBLANES, d), F32),
                        pltpu.VMEM((TM, d), BF16),
                        pltpu.VMEM(w_in.shape[1:], BF16),
                        pltpu.VMEM(w_out.shape[1:], BF16)],
        compiler_params=_params(),
        name="sconv_mixer",
    )(x, g, w_in, conv_w, w_out)


def _swa_kernel(sinks_ref, x_ref, g_ref, wqkv_f32, bqkv_ref, wo_f32, bo_ref, o_ref,
                q_s, k2_s, v2_s, oh_s, bias_s, wqkv_ref, wo_ref, *, tiles_per_seq):
    tm = x_ref.shape[0]
    n_blocks = tm // BLOCK
    group = ATTN_HEADS // ATTN_KV_HEADS
    q_dim = ATTN_HEADS * HEAD_DIM
    kv_dim = ATTN_KV_HEADS * HEAD_DIM
    first = (pl.program_id(0) % tiles_per_seq) == 0

    @pl.when(first)
    def _():
        k2_s[:, :, 0:BLOCK, :] = jnp.zeros((ATTN_KV_HEADS, 2, BLOCK, LANES), BF16)
        v2_s[:, :, 0:BLOCK, :] = jnp.zeros((ATTN_KV_HEADS, 2, BLOCK, LANES), BF16)

    @pl.when(pl.program_id(0) == 0)
    def _():
        _to_bf16(wqkv_f32, wqkv_ref)
        _to_bf16(wo_f32, wo_ref)
        qi = lax.broadcasted_iota(jnp.int32, (BLOCK, 2 * BLOCK), 0)
        kj = lax.broadcasted_iota(jnp.int32, (BLOCK, 2 * BLOCK), 1)
        dist = BLOCK + qi - kj
        neg_inf = jnp.float32(-jnp.inf)
        pen = jnp.where((dist >= 0) & (dist < WINDOW), 0.0, neg_inf)
        pen_start = jnp.where(kj < BLOCK, neg_inf, pen)
        distf = dist.astype(F32)
        for h in range(ATTN_HEADS):
            slope = 2.0 ** (-8.0 * (h + 1) / ATTN_HEADS)
            bias_s[0, h] = pen - slope * distf
            bias_s[1, h] = pen_start - slope * distf

    x = x_ref[...]
    xg, r = _rms_split(x, g_ref[...])
    scale = HEAD_DIM ** -0.5
    for j in range(q_dim // CHUNK):
        lo, hi = j * CHUNK, (j + 1) * CHUNK
        q_s[:, lo:hi] = ((_dot(xg, wqkv_ref[:, lo:hi]) * r + bqkv_ref[:, lo:hi]) * scale).astype(BF16)

    lane = lax.broadcasted_iota(jnp.int32, (tm, LANES), 1)
    for (dst, base) in ((k2_s, q_dim), (v2_s, q_dim + kv_dim)):
        for gi in range(kv_dim // LANES):
            lo = base + gi * LANES
            kg = _dot(xg, wqkv_ref[:, lo:lo + LANES]) * r + bqkv_ref[:, lo:lo + LANES]
            low = jnp.where(lane < HEAD_DIM, kg, 0.0)
            high = jnp.where(lane >= HEAD_DIM, kg, 0.0)
            low_hi = pltpu.roll(low, HEAD_DIM, 1)
            high_lo = pltpu.roll(high, HEAD_DIM, 1)
            dst[2 * gi, 0, BLOCK:, :] = low.astype(BF16)
            dst[2 * gi, 1, BLOCK:, :] = low_hi.astype(BF16)
            dst[2 * gi + 1, 0, BLOCK:, :] = high_lo.astype(BF16)
            dst[2 * gi + 1, 1, BLOCK:, :] = high.astype(BF16)

    for n in range(n_blocks):
        row = n * BLOCK
        variant = jnp.where(first, 1, 0) if n == 0 else 0
        for kh in range(ATTN_KV_HEADS):
            heads = range(kh * group, (kh + 1) * group)
            scores = []
            for h in heads:
                qp = q_s[pl.ds(row, BLOCK), (h // 2) * LANES:(h // 2 + 1) * LANES]
                kb = k2_s[kh, h % 2, pl.ds(row, 2 * BLOCK), :]
                scores.append(_dot_nt(qp, kb))
            outs = []
            for h, s in zip(heads, scores):
                s = s + bias_s[variant, h]
                sink = sinks_ref[h]
                m = jnp.maximum(jnp.max(s, axis=-1, keepdims=True), sink)
                e = jnp.exp(s - m)
                denom = jnp.sum(e, axis=-1, keepdims=True) + jnp.exp(sink - m)
                vb = v2_s[kh, h % 2, pl.ds(row, 2 * BLOCK), :]
                outs.append(_dot(e.astype(BF16), vb) * (1.0 / denom))
            for i in range(group // 2):
                p = (kh * group) // 2 + i
                o_pair = outs[2 * i] + outs[2 * i + 1]
                oh_s[pl.ds(row, BLOCK), p * LANES:(p + 1) * LANES] = o_pair.astype(BF16)

    k2_s[:, :, 0:BLOCK, :] = k2_s[:, :, tm:tm + BLOCK, :]
    v2_s[:, :, 0:BLOCK, :] = v2_s[:, :, tm:tm + BLOCK, :]

    o_ref[...] = x + _dot(oh_s[...], wo_ref[...]) + bo_ref[...]


def _swa_mixer(x, g, w_qkv, b_qkv, sinks, w_o, b_o, *, layer, j, seq):
    n, d = x.shape
    kern = functools.partial(_swa_kernel, tiles_per_seq=seq // TM)
    q_dim = ATTN_HEADS * HEAD_DIM
    return pl.pallas_call(
        kern,
        out_shape=jax.ShapeDtypeStruct((n, d), F32),
        grid=(n // TM,),
        in_specs=[
            pl.BlockSpec(memory_space=pltpu.SMEM),
            pl.BlockSpec((TM, d), lambda i: (i, 0)),
            _layer(g.shape, layer),
            _layer(w_qkv.shape, j),
            _layer(b_qkv.shape, j),
            _layer(w_o.shape, j),
            _layer(b_o.shape, j),
        ],
        out_specs=pl.BlockSpec((TM, d), lambda i: (i, 0)),
        scratch_shapes=[
            pltpu.VMEM((TM, q_dim), BF16),
            pltpu.VMEM((ATTN_KV_HEADS, 2, BLOCK + TM, LANES), BF16),
            pltpu.VMEM((ATTN_KV_HEADS, 2, BLOCK + TM, LANES), BF16),
            pltpu.VMEM((TM, q_dim), BF16),
            pltpu.VMEM((2, ATTN_HEADS, BLOCK, 2 * BLOCK), F32),
            pltpu.VMEM(w_qkv.shape[1:], BF16),
            pltpu.VMEM(w_o.shape[1:], BF16),
        ],
        compiler_params=_params(),
        name="swa_mixer",
    )(sinks, x, g, w_qkv, b_qkv, w_o, b_o)


def _conformer_kernel(x_ref, g_ref, w1_f32, b1_ref, wdw_ref, bdw_ref, lng_ref, lnb_ref,
                      w2_f32, b2_ref, o_ref, ext_s, u_s, w1_ref, w2_ref, *, tiles_per_seq):
    tm = x_ref.shape[0]
    n_slabs = D_MODEL // LANES
    rows = tm // CONF_STRIDE

    @pl.when(pl.program_id(0) == 0)
    def _():
        _to_bf16(w1_f32, w1_ref)
        _to_bf16(w2_f32, w2_ref)

    first = (pl.program_id(0) % tiles_per_seq) == 0
    x = x_ref[...]
    xg, r = _rms_split(x, g_ref[...])
    for j in range(D_MODEL // CHUNK):
        lo, hi = j * CHUNK, (j + 1) * CHUNK
        a = _dot(xg, w1_ref[:, lo:hi]) * r + b1_ref[:, lo:hi]
        gate = _dot(xg, w1_ref[:, D_MODEL + lo:D_MODEL + hi]) * r + b1_ref[:, D_MODEL + lo:D_MODEL + hi]
        u = a * _sigmoid(gate)
        for i in range(CHUNK // LANES):
            k = j * (CHUNK // LANES) + i
            klo, khi = k * LANES, (k + 1) * LANES
            ext_s[k, 0:CONF_CARRY, :] = jnp.where(first, 0.0, ext_s[k, tm:tm + CONF_CARRY, :])
            ext_s[k, CONF_CARRY:CONF_CARRY + tm, :] = u[:, i * LANES:(i + 1) * LANES]
            for part in range(CONF_PARTS):
                n = rows // CONF_PARTS
                base = part * n * CONF_STRIDE
                acc = [jnp.zeros((n, LANES), F32) + bdw_ref[:, klo:khi] for _ in range(CONF_STRIDE)]
                for d in range(CONF_STRIDE - 1, -CONF_WIDTH, -1):
                    src = ext_s[k, pl.ds(CONF_CARRY + base + d, n, stride=CONF_STRIDE), :]
                    for p in range(CONF_STRIDE):
                        delay = p - d
                        if 0 <= delay < CONF_WIDTH:
                            tap = CONF_WIDTH - 1 - delay
                            acc[p] = acc[p] + wdw_ref[tap:tap + 1, klo:khi] * src
                for p in range(CONF_STRIDE):
                    u_s[k, pl.ds(base + p, n, stride=CONF_STRIDE), :] = acc[p]
    u = jnp.concatenate([u_s[k] for k in range(n_slabs)], axis=1)
    mu = jnp.mean(u, axis=-1, keepdims=True)
    uc = u - mu
    y = uc * lax.rsqrt(jnp.mean(uc * uc, axis=-1, keepdims=True) + EPS)
    y = y * lng_ref[...] + lnb_ref[...]
    y = (y * _sigmoid(y)).astype(BF16)
    o_ref[...] = x + _dot(y, w2_ref[...]) + b2_ref[...]


def _conformer_mixer(x, g, w1, b1, wdw, bdw, lng, lnb, w2, b2, *, layer, j, seq):
    n, d = x.shape
    kern = functools.partial(_conformer_kernel, tiles_per_seq=seq // TM)
    return pl.pallas_call(
        kern,
        out_shape=jax.ShapeDtypeStruct((n, d), F32),
        grid=(n // TM,),
        in_specs=[pl.BlockSpec((TM, d), lambda i: (i, 0))]
        + [_layer(g.shape, layer)]
        + [_layer(a.shape, j) for a in (w1, b1, wdw, bdw, lng, lnb, w2, b2)],
        out_specs=pl.BlockSpec((TM, d), lambda i: (i, 0)),
        scratch_shapes=[pltpu.VMEM((d // LANES, CONF_CARRY + TM, LANES), F32),
                        pltpu.VMEM((d // LANES, TM, LANES), F32),
                        pltpu.VMEM(w1.shape[1:], BF16),
                        pltpu.VMEM(w2.shape[1:], BF16)],
        compiler_params=_params(),
        name="conformer_mixer",
    )(x, g, w1, b1, wdw, bdw, lng, lnb, w2, b2)


def _memfold_kernel(mem_ref, g_ref, wkv_ref, wq_ref, wo_ref, m_ref, vw_ref):
    d = mem_ref.shape[-1]
    hd = XATTN_HEAD_DIM
    mn = _rms(mem_ref[...], g_ref[...]).astype(BF16)
    k = _dot(mn, wkv_ref[:, :d].astype(BF16)).astype(BF16)
    v = _dot(mn, wkv_ref[:, d:].astype(BF16)).astype(BF16)
    scale = hd ** -0.5
    for h in range(XATTN_HEADS):
        lo, hi = h * hd, (h + 1) * hd
        m_ref[:, lo:hi] = (_dot_nt(wq_ref[:, lo:hi].astype(BF16), k[:, lo:hi]) * scale).astype(BF16)
        vw_ref[lo:hi, :] = _dot(v[:, lo:hi], wo_ref[lo:hi, :].astype(BF16)).astype(BF16)


def _memory_fold(mem, g_mem, w_kv, w_q, w_o):
    bsz, m, d = mem.shape
    depth = w_kv.shape[0]
    hm = XATTN_HEADS * m
    return pl.pallas_call(
        _memfold_kernel,
        out_shape=(jax.ShapeDtypeStruct((depth, bsz, d, hm), BF16),
                   jax.ShapeDtypeStruct((depth, bsz, hm, d), BF16)),
        grid=(depth, bsz),
        in_specs=[
            pl.BlockSpec((None, m, d), lambda l, b: (b, 0, 0)),
            pl.BlockSpec((None, 1, d), lambda l, b: (l, 0, 0)),
            pl.BlockSpec((None, d, 2 * d), lambda l, b: (l, 0, 0)),
            pl.BlockSpec((None, d, d), lambda l, b: (l, 0, 0)),
            pl.BlockSpec((None, d, d), lambda l, b: (l, 0, 0)),
        ],
        out_specs=(pl.BlockSpec((None, None, d, hm), lambda l, b: (l, b, 0, 0)),
                   pl.BlockSpec((None, None, hm, d), lambda l, b: (l, b, 0, 0))),
        compiler_params=pltpu.CompilerParams(
            dimension_semantics=("arbitrary", "arbitrary"),
            vmem_limit_bytes=VMEM_LIMIT_BYTES),
        name="memory_fold",
    )(mem, g_mem, w_kv, w_q, w_o)


def _xattn_tile(x, g_ref, m_ref, vw_ref, p_s):
    n_mem = m_ref.shape[-1] // XATTN_HEADS
    xg, r = _rms_split(x, g_ref[...])
    scores = [_dot(xg, m_ref[:, h * n_mem:(h + 1) * n_mem]) for h in range(XATTN_HEADS)]
    for h, s in enumerate(scores):
        s = s * r
        e = jnp.exp(s - jnp.max(s, axis=-1, keepdims=True))
        inv = 1.0 / jnp.sum(e, axis=-1, keepdims=True)
        p_s[:, h * n_mem:(h + 1) * n_mem] = (e * inv).astype(BF16)
    return x + _dot(p_s[...], vw_ref[...])


FFN_PREP = D_FF // CHUNK
UP_SLAB = 2 * D_FF // FFN_PREP


def _up_cols(ref, col):
    slab, off = divmod(col, UP_SLAB)
    return ref[slab, :, off:off + CHUNK]


def _xattn_ffn_kernel(x_ref, gx_ref, m_ref, vw_ref, g_ref, wup_slab, cw_ref, wd_slab, gf_ref,
                      o_ref, p_s, act_s, carry_s, wup_s, wd_s, *, tiles_per_seq, final_norm):
    tm = x_ref.shape[0]
    step = pl.program_id(0)

    @pl.when(step < FFN_PREP)
    def _():
        wup_s[step] = wup_slab[...].astype(BF16)
        row = pl.multiple_of(step * CHUNK, CHUNK)
        wd_s[pl.ds(row, CHUNK), :] = wd_slab[...].astype(BF16)

    @pl.when(step >= FFN_PREP)
    def _():
        first = ((step - FFN_PREP) % tiles_per_seq) == 0
        x = _xattn_tile(x_ref[...], gx_ref, m_ref, vw_ref, p_s)
        xg, r = _rms_split(x, g_ref[...])
        for j in range(D_FF // CHUNK):
            lo, hi = j * CHUNK, (j + 1) * CHUNK
            gate = _dot(xg, _up_cols(wup_s, lo)) * r
            up = _dot(xg, _up_cols(wup_s, D_FF + lo)) * r
            prev = jnp.where(first, 0.0, carry_s[:, lo:hi])
            carry_s[:, lo:hi] = gate[tm - SUBLANES:, :]
            cv = (cw_ref[2:3, lo:hi] * gate
                  + cw_ref[1:2, lo:hi] * _shifted(prev, gate, 1)
                  + cw_ref[0:1, lo:hi] * _shifted(prev, gate, 2))
            act_s[:, lo:hi] = (cv * _sigmoid(cv) * up).astype(BF16)
        y = x + _dot(act_s[...], wd_s[...])
        if final_norm:
            y = _rms(y, gf_ref[...])
        o_ref[...] = y


def _xattn_mlp(x, gx, m_all, vw_all, g, w_up, cw, wd, g_final, *, layer, seq, final_norm):
    n, d = x.shape
    hm = m_all.shape[-1]
    tiles_per_seq = seq // TM
    assert UP_SLAB % CHUNK == 0 and D_FF % UP_SLAB == CHUNK % UP_SLAB
    kern = functools.partial(_xattn_ffn_kernel, tiles_per_seq=tiles_per_seq, final_norm=final_norm)
    last = FFN_PREP - 1
    tile = lambda i: (jnp.maximum(i - FFN_PREP, 0), 0)
    batch = lambda i: (layer, jnp.maximum(i - FFN_PREP, 0) // tiles_per_seq, 0, 0)
    return pl.pallas_call(
        kern,
        out_shape=jax.ShapeDtypeStruct((n, d), F32),
        grid=(FFN_PREP + n // TM,),
        in_specs=[
            pl.BlockSpec((TM, d), tile),
            _layer(gx.shape, layer),
            pl.BlockSpec((None, None, d, hm), batch),
            pl.BlockSpec((None, None, hm, d), batch),
            _layer(g.shape, layer),
            pl.BlockSpec((None, d, UP_SLAB), lambda i: (layer, 0, jnp.minimum(i, last))),
            _layer(cw.shape, layer),
            pl.BlockSpec((None, CHUNK, d), lambda i: (layer, jnp.minimum(i, last), 0)),
            _resident(g_final.shape),
        ],
        out_specs=pl.BlockSpec((TM, d), tile),
        scratch_shapes=[pltpu.VMEM((TM, hm), BF16),
                        pltpu.VMEM((TM, D_FF), BF16),
                        pltpu.VMEM((SUBLANES, D_FF), F32),
                        pltpu.VMEM((FFN_PREP, d, UP_SLAB), BF16),
                        pltpu.VMEM((D_FF, d), BF16)],
        compiler_params=_params(),
        name="xattn_mlp",
    )(x, gx, m_all, vw_all, g, w_up, cw, wd, g_final)


def kernel(x, mem, norm_mix, norm_xattn, norm_mem, norm_ffn, norm_final, a_w_in, a_conv, a_w_out, b_w_qkv, b_b_qkv, b_sinks, b_w_o, b_b_o, c_w_pw1, c_b_pw1, c_w_dw, c_b_dw, c_ln_g, c_ln_b, c_w_pw2, c_b_pw2, x_w_q, x_w_kv, x_w_o, f_w_up, f_conv, f_w_down):
    bsz, seq, d = x.shape
    depth = norm_mix.shape[0]
    assert d == D_MODEL and seq % TM == 0 and TM % BLOCK == 0
    row = lambda p: p.reshape(p.shape[0], 1, p.shape[1])

    h = x.reshape(bsz * seq, d)
    m_all, vw_all = _memory_fold(mem, row(norm_mem), x_w_kv, x_w_q, x_w_o)
    g_mix, g_xattn, g_ffn = row(norm_mix), row(norm_xattn), row(norm_ffn)

    for i in range(depth):
        kind, j = i % 3, i // 3
        if kind == 0:
            h = _sconv_mixer(h, g_mix, a_w_in, a_conv, a_w_out, layer=i, j=j, seq=seq)
        elif kind == 1:
            h = _swa_mixer(h, g_mix, b_w_qkv, row(b_b_qkv), b_sinks[j], b_w_o, row(b_b_o),
                           layer=i, j=j, seq=seq)
        else:
            h = _conformer_mixer(h, g_mix, c_w_pw1, row(c_b_pw1), c_w_dw, row(c_b_dw),
                                 row(c_ln_g), row(c_ln_b), c_w_pw2, row(c_b_pw2),
                                 layer=i, j=j, seq=seq)
        h = _xattn_mlp(h, g_xattn, m_all, vw_all, g_ffn, f_w_up, f_conv, f_w_down,
                       norm_final.reshape(1, d), layer=i, seq=seq, final_norm=(i == depth - 1))
    return h.reshape(bsz, seq, d)
```

```python
import functools

import jax
import jax.numpy as jnp
from jax import lax
from jax.experimental import pallas as pl
from jax.experimental.pallas import tpu as pltpu

EPS = 1e-6
F32 = jnp.float32
BF16 = jnp.bfloat16

D_MODEL = 1024
SC_WIDTH = 3
ATTN_HEADS = 16
ATTN_KV_HEADS = 4
HEAD_DIM = 64
WINDOW = 128
BLOCK = 128
CONF_WIDTH = 31
XATTN_HEADS = 4
XATTN_HEAD_DIM = D_MODEL // XATTN_HEADS
D_FF = 2816
FFN_CONV_WIDTH = 3

SUBLANES = 8
LANES = 128
MXU_DIM = 256
VMEM_LIMIT_BYTES = 56 * 1024 * 1024

TM = 512
CHUNK = MXU_DIM
CONF_CARRY = 32
CONF_STRIDE = 4
CONF_PARTS = 2


def _rms(x, g):
    ms = jnp.mean(x * x, axis=-1, keepdims=True)
    return x * lax.rsqrt(ms + EPS) * g


def _rms_split(x, g):
    r = lax.rsqrt(jnp.mean(x * x, axis=-1, keepdims=True) + EPS)
    return (x * g).astype(BF16), r


def _dot(a, b):
    return jnp.dot(a, b, preferred_element_type=F32)


def _dot_nt(a, b):
    return lax.dot_general(a, b, (((1,), (1,)), ((), ())), preferred_element_type=F32)


def _sigmoid(x):
    return 1.0 / (1.0 + jnp.exp(-x))


def _shifted(prev, cur, shift):
    n_prev = prev.shape[0]
    ext = jnp.concatenate([prev, cur], axis=0)
    return pltpu.roll(ext, shift, 0)[n_prev:]


def _to_bf16(src_ref, dst_ref, rows=256):
    for r in range(0, src_ref.shape[0], rows):
        dst_ref[r:r + rows, :] = src_ref[r:r + rows, :].astype(BF16)


def _resident(shape):
    nd = len(shape)
    return pl.BlockSpec(shape, lambda *_: (0,) * nd, pipeline_mode=pl.Buffered(1))


def _layer(shape, layer):
    nd = len(shape) - 1
    return pl.BlockSpec((None,) + tuple(shape[1:]), lambda *_: (layer,) + (0,) * nd,
                        pipeline_mode=pl.Buffered(1))


def _params():
    return pltpu.CompilerParams(dimension_semantics=("arbitrary",),
                                vmem_limit_bytes=VMEM_LIMIT_BYTES)


def _sconv_kernel(x_ref, g_ref, win_f32, cw_ref, wout_f32, o_ref, carry_ref, z_ref,
                  win_ref, wout_ref, *, tiles_per_seq):
    tm = x_ref.shape[0]

    @pl.when(pl.program_id(0) == 0)
    def _():
        _to_bf16(win_f32, win_ref)
        _to_bf16(wout_f32, wout_ref)

    first = (pl.program_id(0) % tiles_per_seq) == 0
    x = x_ref[...]
    xg, r = _rms_split(x, g_ref[...])
    cw = cw_ref[...]
    for j in range(D_MODEL // CHUNK):
        lo, hi = j * CHUNK, (j + 1) * CHUNK
        b = _dot(xg, win_ref[:, lo:hi]) * r
        c = _dot(xg, win_ref[:, D_MODEL + lo:D_MODEL + hi]) * r
        v = _dot(xg, win_ref[:, 2 * D_MODEL + lo:2 * D_MODEL + hi]) * r
        u = c * v
        prev = jnp.where(first, 0.0, carry_ref[:, lo:hi])
        carry_ref[:, lo:hi] = u[tm - SUBLANES:, :]
        conv = (cw[2:3, lo:hi] * u
                + cw[1:2, lo:hi] * _shifted(prev, u, 1)
                + cw[0:1, lo:hi] * _shifted(prev, u, 2))
        z_ref[:, lo:hi] = (b * conv).astype(BF16)
    o_ref[...] = x + _dot(z_ref[...], wout_ref[...])


def _sconv_mixer(x, g, w_in, conv_w, w_out, *, layer, j, seq):
    n, d = x.shape
    kern = functools.partial(_sconv_kernel, tiles_per_seq=seq // TM)
    return pl.pallas_call(
        kern,
        out_shape=jax.ShapeDtypeStruct((n, d), F32),
        grid=(n // TM,),
        in_specs=[
            pl.BlockSpec((TM, d), lambda i: (i, 0)),
            _layer(g.shape, layer),
            _layer(w_in.shape, j),
            _layer(conv_w.shape, j),
            _layer(w_out.shape, j),
        ],
        out_specs=pl.BlockSpec((TM, d), lambda i: (i, 0)),
        scratch_shapes=[pltpu.VMEM((SUBLANES, d), F32),
                        pltpu.VMEM((TM, d), BF16),
                        pltpu.VMEM(w_in.shape[1:], BF16),
                        pltpu.VMEM(w_out.shape[1:], BF16)],
        compiler_params=_params(),
        name="sconv_mixer",
    )(x, g, w_in, conv_w, w_out)


def _swa_kernel(sinks_ref, x_ref, g_ref, wqkv_f32, bqkv_ref, wo_f32, bo_ref, o_ref,
                q_s, k2_s, v2_s, oh_s, bias_s, wqkv_ref, wo_ref, *, tiles_per_seq):
    tm = x_ref.shape[0]
    n_blocks = tm // BLOCK
    group = ATTN_HEADS // ATTN_KV_HEADS
    q_dim = ATTN_HEADS * HEAD_DIM
    kv_dim = ATTN_KV_HEADS * HEAD_DIM
    first = (pl.program_id(0) % tiles_per_seq) == 0

    @pl.when(first)
    def _():
        k2_s[:, :, 0:BLOCK, :] = jnp.zeros((ATTN_KV_HEADS, 2, BLOCK, LANES), BF16)
        v2_s[:, :, 0:BLOCK, :] = jnp.zeros((ATTN_KV_HEADS, 2, BLOCK, LANES), BF16)

    @pl.when(pl.program_id(0) == 0)
    def _():
        _to_bf16(wqkv_f32, wqkv_ref)
        _to_bf16(wo_f32, wo_ref)
        qi = lax.broadcasted_iota(jnp.int32, (BLOCK, 2 * BLOCK), 0)
        kj = lax.broadcasted_iota(jnp.int32, (BLOCK, 2 * BLOCK), 1)
        dist = BLOCK + qi - kj
        neg_inf = jnp.float32(-jnp.inf)
        pen = jnp.where((dist >= 0) & (dist < WINDOW), 0.0, neg_inf)
        pen_start = jnp.where(kj < BLOCK, neg_inf, pen)
        distf = dist.astype(F32)
        for h in range(ATTN_HEADS):
            slope = 2.0 ** (-8.0 * (h + 1) / ATTN_HEADS)
            bias_s[0, h] = pen - slope * distf
            bias_s[1, h] = pen_start - slope * distf

    x = x_ref[...]
    xg, r = _rms_split(x, g_ref[...])
    scale = HEAD_DIM ** -0.5
    for j in range(q_dim // CHUNK):
        lo, hi = j * CHUNK, (j + 1) * CHUNK
        q_s[:, lo:hi] = ((_dot(xg, wqkv_ref[:, lo:hi]) * r + bqkv_ref[:, lo:hi]) * scale).astype(BF16)

    lane = lax.broadcasted_iota(jnp.int32, (tm, LANES), 1)
    for (dst, base) in ((k2_s, q_dim), (v2_s, q_dim + kv_dim)):
        for gi in range(kv_dim // LANES):
            lo = base + gi * LANES
            kg = _dot(xg, wqkv_ref[:, lo:lo + LANES]) * r + bqkv_ref[:, lo:lo + LANES]
            low = jnp.where(lane < HEAD_DIM, kg, 0.0)
            high = jnp.where(lane >= HEAD_DIM, kg, 0.0)
            low_hi = pltpu.roll(low, HEAD_DIM, 1)
            high_lo = pltpu.roll(high, HEAD_DIM, 1)
            dst[2 * gi, 0, BLOCK:, :] = low.astype(BF16)
            dst[2 * gi, 1, BLOCK:, :] = low_hi.astype(BF16)
            dst[2 * gi + 1, 0, BLOCK:, :] = high_lo.astype(BF16)
            dst[2 * gi + 1, 1, BLOCK:, :] = high.astype(BF16)

    for n in range(n_blocks):
        row = n * BLOCK
        variant = jnp.where(first, 1, 0) if n == 0 else 0
        for kh in range(ATTN_KV_HEADS):
            heads = range(kh * group, (kh + 1) * group)
            scores = []
            for h in heads:
                qp = q_s[pl.ds(row, BLOCK), (h // 2) * LANES:(h // 2 + 1) * LANES]
                kb = k2_s[kh, h % 2, pl.ds(row, 2 * BLOCK), :]
                scores.append(_dot_nt(qp, kb))
            outs = []
            for h, s in zip(heads, scores):
                s = s + bias_s[variant, h]
                sink = sinks_ref[h]
                m = jnp.maximum(jnp.max(s, axis=-1, keepdims=True), sink)
                e = jnp.exp(s - m)
                denom = jnp.sum(e, axis=-1, keepdims=True) + jnp.exp(sink - m)
                vb = v2_s[kh, h % 2, pl.ds(row, 2 * BLOCK), :]
                outs.append(_dot(e.astype(BF16), vb) * (1.0 / denom))
            for i in range(group // 2):
                p = (kh * group) // 2 + i
                o_pair = outs[2 * i] + outs[2 * i + 1]
                oh_s[pl.ds(row, BLOCK), p * LANES:(p + 1) * LANES] = o_pair.astype(BF16)

    k2_s[:, :, 0:BLOCK, :] = k2_s[:, :, tm:tm + BLOCK, :]
    v2_s[:, :, 0:BLOCK, :] = v2_s[:, :, tm:tm + BLOCK, :]

    o_ref[...] = x + _dot(oh_s[...], wo_ref[...]) + bo_ref[...]


def _swa_mixer(x, g, w_qkv, b_qkv, sinks, w_o, b_o, *, layer, j, seq):
    n, d = x.shape
    kern = functools.partial(_swa_kernel, tiles_per_seq=seq // TM)
    q_dim = ATTN_HEADS * HEAD_DIM
    return pl.pallas_call(
        kern,
        out_shape=jax.ShapeDtypeStruct((n, d), F32),
        grid=(n // TM,),
        in_specs=[
            pl.BlockSpec(memory_space=pltpu.SMEM),
            pl.BlockSpec((TM, d), lambda i: (i, 0)),
            _layer(g.shape, layer),
            _layer(w_qkv.shape, j),
            _layer(b_qkv.shape, j),
            _layer(w_o.shape, j),
            _layer(b_o.shape, j),
        ],
        out_specs=pl.BlockSpec((TM, d), lambda i: (i, 0)),
        scratch_shapes=[
            pltpu.VMEM((TM, q_dim), BF16),
            pltpu.VMEM((ATTN_KV_HEADS, 2, BLOCK + TM, LANES), BF16),
            pltpu.VMEM((ATTN_KV_HEADS, 2, BLOCK + TM, LANES), BF16),
            pltpu.VMEM((TM, q_dim), BF16),
            pltpu.VMEM((2, ATTN_HEADS, BLOCK, 2 * BLOCK), F32),
            pltpu.VMEM(w_qkv.shape[1:], BF16),
            pltpu.VMEM(w_o.shape[1:], BF16),
        ],
        compiler_params=_params(),
        name="swa_mixer",
    )(sinks, x, g, w_qkv, b_qkv, w_o, b_o)


def _conformer_kernel(x_ref, g_ref, w1_f32, b1_ref, wdw_ref, bdw_ref, lng_ref, lnb_ref,
                      w2_f32, b2_ref, o_ref, ext_s, u_s, w1_ref, w2_ref, *, tiles_per_seq):
    tm = x_ref.shape[0]
    n_slabs = D_MODEL // LANES
    rows = tm // CONF_STRIDE

    @pl.when(pl.program_id(0) == 0)
    def _():
        _to_bf16(w1_f32, w1_ref)
        _to_bf16(w2_f32, w2_ref)

    first = (pl.program_id(0) % tiles_per_seq) == 0
    x = x_ref[...]
    xg, r = _rms_split(x, g_ref[...])
    for j in range(D_MODEL // CHUNK):
        lo, hi = j * CHUNK, (j + 1) * CHUNK
        a = _dot(xg, w1_ref[:, lo:hi]) * r + b1_ref[:, lo:hi]
        gate = _dot(xg, w1_ref[:, D_MODEL + lo:D_MODEL + hi]) * r + b1_ref[:, D_MODEL + lo:D_MODEL + hi]
        u = a * _sigmoid(gate)
        for i in range(CHUNK // LANES):
            k = j * (CHUNK // LANES) + i
            klo, khi = k * LANES, (k + 1) * LANES
            ext_s[k, 0:CONF_CARRY, :] = jnp.where(first, 0.0, ext_s[k, tm:tm + CONF_CARRY, :])
            ext_s[k, CONF_CARRY:CONF_CARRY + tm, :] = u[:, i * LANES:(i + 1) * LANES]
            for part in range(CONF_PARTS):
                n = rows // CONF_PARTS
                base = part * n * CONF_STRIDE
                acc = [jnp.zeros((n, LANES), F32) + bdw_ref[:, klo:khi] for _ in range(CONF_STRIDE)]
                for d in range(CONF_STRIDE - 1, -CONF_WIDTH, -1):
                    src = ext_s[k, pl.ds(CONF_CARRY + base + d, n, stride=CONF_STRIDE), :]
                    for p in range(CONF_STRIDE):
                        delay = p - d
                        if 0 <= delay < CONF_WIDTH:
                            tap = CONF_WIDTH - 1 - delay
                            acc[p] = acc[p] + wdw_ref[tap:tap + 1, klo:khi] * src
                for p in range(CONF_STRIDE):
                    u_s[k, pl.ds(base + p, n, stride=CONF_STRIDE), :] = acc[p]
    u = jnp.concatenate([u_s[k] for k in range(n_slabs)], axis=1)
    mu = jnp.mean(u, axis=-1, keepdims=True)
    uc = u - mu
    y = uc * lax.rsqrt(jnp.mean(uc * uc, axis=-1, keepdims=True) + EPS)
    y = y * lng_ref[...] + lnb_ref[...]
    y = (y * _sigmoid(y)).astype(BF16)
    o_ref[...] = x + _dot(y, w2_ref[...]) + b2_ref[...]


def _conformer_mixer(x, g, w1, b1, wdw, bdw, lng, lnb, w2, b2, *, layer, j, seq):
    n, d = x.shape
    kern = functools.partial(_conformer_kernel, tiles_per_seq=seq // TM)
    return pl.pallas_call(
        kern,
        out_shape=jax.ShapeDtypeStruct((n, d), F32),
        grid=(n // TM,),
        in_specs=[pl.BlockSpec((TM, d), lambda i: (i, 0))]
        + [_layer(g.shape, layer)]
        + [_layer(a.shape, j) for a in (w1, b1, wdw, bdw, lng, lnb, w2, b2)],
        out_specs=pl.BlockSpec((TM, d), lambda i: (i, 0)),
        scratch_shapes=[pltpu.VMEM((d // LANES, CONF_CARRY + TM, LANES), F32),
                        pltpu.VMEM((d // LANES, TM, LANES), F32),
                        pltpu.VMEM(w1.shape[1:], BF16),
                        pltpu.VMEM(w2.shape[1:], BF16)],
        compiler_params=_params(),
        name="conformer_mixer",
    )(x, g, w1, b1, wdw, bdw, lng, lnb, w2, b2)


def _memfold_kernel(mem_ref, g_ref, wk_ref, wv_ref, wq_ref, wo_ref, m_ref, vw_ref):
    mn = _rms(mem_ref[...], g_ref[...]).astype(BF16)
    k = _dot(mn, wk_ref[...].astype(BF16)).astype(BF16)
    v = _dot(mn, wv_ref[...].astype(BF16)).astype(BF16)
    scale = XATTN_HEAD_DIM ** -0.5
    m_ref[...] = (_dot_nt(wq_ref[...].astype(BF16), k) * scale).astype(BF16)
    vw_ref[...] = _dot(v, wo_ref[...].astype(BF16)).astype(BF16)


def _memory_fold(mem, g_mem, w_kv, w_q, w_o):
    bsz, m, d = mem.shape
    depth = w_kv.shape[0]
    hd = XATTN_HEAD_DIM
    hm = XATTN_HEADS * m
    return pl.pallas_call(
        _memfold_kernel,
        out_shape=(jax.ShapeDtypeStruct((depth, bsz, d, hm), BF16),
                   jax.ShapeDtypeStruct((depth, bsz, hm, d), BF16)),
        grid=(depth, XATTN_HEADS, bsz),
        in_specs=[
            pl.BlockSpec((None, m, d), lambda l, h, b: (b, 0, 0)),
            pl.BlockSpec((None, 1, d), lambda l, h, b: (l, 0, 0)),
            pl.BlockSpec((None, d, hd), lambda l, h, b: (l, 0, h)),
            pl.BlockSpec((None, d, hd), lambda l, h, b: (l, 0, XATTN_HEADS + h)),
            pl.BlockSpec((None, d, hd), lambda l, h, b: (l, 0, h)),
            pl.BlockSpec((None, hd, d), lambda l, h, b: (l, h, 0)),
        ],
        out_specs=(pl.BlockSpec((None, None, d, m), lambda l, h, b: (l, b, 0, h)),
                   pl.BlockSpec((None, None, m, d), lambda l, h, b: (l, b, h, 0))),
        compiler_params=pltpu.CompilerParams(
            dimension_semantics=("arbitrary", "arbitrary", "arbitrary"),
            vmem_limit_bytes=VMEM_LIMIT_BYTES),
        name="memory_fold",
    )(mem, g_mem, w_kv, w_kv, w_q, w_o)


def _xattn_tile(x, g_ref, m_ref, vw_ref, p_s):
    n_mem = m_ref.shape[-1] // XATTN_HEADS
    xg, r = _rms_split(x, g_ref[...])
    scores = [_dot(xg, m_ref[:, h * n_mem:(h + 1) * n_mem]) for h in range(XATTN_HEADS)]
    for h, s in enumerate(scores):
        s = s * r
        e = jnp.exp(s - jnp.max(s, axis=-1, keepdims=True))
        inv = 1.0 / jnp.sum(e, axis=-1, keepdims=True)
        p_s[:, h * n_mem:(h + 1) * n_mem] = (e * inv).astype(BF16)
    return x + _dot(p_s[...], vw_ref[...])


FFN_PREP = D_FF // CHUNK
UP_SLAB = 2 * D_FF // FFN_PREP


def _up_cols(ref, col):
    slab, off = divmod(col, UP_SLAB)
    return ref[slab, :, off:off + CHUNK]


def _xattn_ffn_kernel(x_ref, gx_ref, m_ref, vw_ref, g_ref, wup_slab, cw_ref, wd_slab, gf_ref,
                      o_ref, p_s, act_s, carry_s, wup_s, wd_s, *, tiles_per_seq, final_norm):
    tm = x_ref.shape[0]
    step = pl.program_id(0)

    @pl.when(step < FFN_PREP)
    def _():
        wup_s[step] = wup_slab[...].astype(BF16)
        row = pl.multiple_of(step * CHUNK, CHUNK)
        wd_s[pl.ds(row, CHUNK), :] = wd_slab[...].astype(BF16)

    @pl.when(step >= FFN_PREP)
    def _():
        first = ((step - FFN_PREP) % tiles_per_seq) == 0
        x = _xattn_tile(x_ref[...], gx_ref, m_ref, vw_ref, p_s)
        xg, r = _rms_split(x, g_ref[...])
        for j in range(D_FF // CHUNK):
            lo, hi = j * CHUNK, (j + 1) * CHUNK
            gate = _dot(xg, _up_cols(wup_s, lo)) * r
            up = _dot(xg, _up_cols(wup_s, D_FF + lo)) * r
            prev = jnp.where(first, 0.0, carry_s[:, lo:hi])
            carry_s[:, lo:hi] = gate[tm - SUBLANES:, :]
            cv = (cw_ref[2:3, lo:hi] * gate
                  + cw_ref[1:2, lo:hi] * _shifted(prev, gate, 1)
                  + cw_ref[0:1, lo:hi] * _shifted(prev, gate, 2))
            act_s[:, lo:hi] = (cv * _sigmoid(cv) * up).astype(BF16)
        y = x + _dot(act_s[...], wd_s[...])
        if final_norm:
            y = _rms(y, gf_ref[...])
        o_ref[...] = y


def _xattn_mlp(x, gx, m_all, vw_all, g, w_up, cw, wd, g_final, *, layer, seq, final_norm):
    n, d = x.shape
    hm = m_all.shape[-1]
    tiles_per_seq = seq // TM
    assert UP_SLAB % CHUNK == 0 and D_FF % UP_SLAB == CHUNK % UP_SLAB
    kern = functools.partial(_xattn_ffn_kernel, tiles_per_seq=tiles_per_seq, final_norm=final_norm)
    last = FFN_PREP - 1
    tile = lambda i: (jnp.maximum(i - FFN_PREP, 0), 0)
    batch = lambda i: (layer, jnp.maximum(i - FFN_PREP, 0) // tiles_per_seq, 0, 0)
    return pl.pallas_call(
        kern,
        out_shape=jax.ShapeDtypeStruct((n, d), F32),
        grid=(FFN_PREP + n // TM,),
        in_specs=[
            pl.BlockSpec((TM, d), tile),
            _layer(gx.shape, layer),
            pl.BlockSpec((None, None, d, hm), batch),
            pl.BlockSpec((None, None, hm, d), batch),
            _layer(g.shape, layer),
            pl.BlockSpec((None, d, UP_SLAB), lambda i: (layer, 0, jnp.minimum(i, last))),
            _layer(cw.shape, layer),
            pl.BlockSpec((None, CHUNK, d), lambda i: (layer, jnp.minimum(i, last), 0)),
            _resident(g_final.shape),
        ],
        out_specs=pl.BlockSpec((TM, d), tile),
        scratch_shapes=[pltpu.VMEM((TM, hm), BF16),
                        pltpu.VMEM((TM, D_FF), BF16),
                        pltpu.VMEM((SUBLANES, D_FF), F32),
                        pltpu.VMEM((FFN_PREP, d, UP_SLAB), BF16),
                        pltpu.VMEM((D_FF, d), BF16)],
        compiler_params=_params(),
        name="xattn_mlp",
    )(x, gx, m_all, vw_all, g, w_up, cw, wd, g_final)


def kernel(x, mem, norm_mix, norm_xattn, norm_mem, norm_ffn, norm_final, a_w_in, a_conv, a_w_out, b_w_qkv, b_b_qkv, b_sinks, b_w_o, b_b_o, c_w_pw1, c_b_pw1, c_w_dw, c_b_dw, c_ln_g, c_ln_b, c_w_pw2, c_b_pw2, x_w_q, x_w_kv, x_w_o, f_w_up, f_conv, f_w_down):
    bsz, seq, d = x.shape
    depth = norm_mix.shape[0]
    assert d == D_MODEL and seq % TM == 0 and TM % BLOCK == 0
    row = lambda p: p.reshape(p.shape[0], 1, p.shape[1])

    h = x.reshape(bsz * seq, d)
    m_all, vw_all = _memory_fold(mem, row(norm_mem), x_w_kv, x_w_q, x_w_o)
    g_mix, g_xattn, g_ffn = row(norm_mix), row(norm_xattn), row(norm_ffn)

    for i in range(depth):
        kind, j = i % 3, i // 3
        if kind == 0:
            h = _sconv_mixer(h, g_mix, a_w_in, a_conv, a_w_out, layer=i, j=j, seq=seq)
        elif kind == 1:
            h = _swa_mixer(h, g_mix, b_w_qkv, row(b_b_qkv), b_sinks[j], b_w_o, row(b_b_o),
                           layer=i, j=j, seq=seq)
        else:
            h = _conformer_mixer(h, g_mix, c_w_pw1, row(c_b_pw1), c_w_dw, row(c_b_dw),
                                 row(c_ln_g), row(c_ln_b), c_w_pw2, row(c_b_pw2),
                                 layer=i, j=j, seq=seq)
        h = _xattn_mlp(h, g_xattn, m_all, vw_all, g_ffn, f_w_up, f_conv, f_w_down,
                       norm_final.reshape(1, d), layer=i, seq=seq, final_norm=(i == depth - 1))
    return h.reshape(bsz, seq, d)
```

```python
import functools

import jax
import jax.numpy as jnp
from jax import lax
from jax.experimental import pallas as pl
from jax.experimental.pallas import tpu as pltpu

EPS = 1e-6
F32 = jnp.float32
BF16 = jnp.bfloat16

D_MODEL = 1024
SC_WIDTH = 3
ATTN_HEADS = 16
ATTN_KV_HEADS = 4
HEAD_DIM = 64
WINDOW = 128
BLOCK = 128
CONF_WIDTH = 31
XATTN_HEADS = 4
XATTN_HEAD_DIM = D_MODEL // XATTN_HEADS
D_FF = 2816
FFN_CONV_WIDTH = 3

SUBLANES = 8
LANES = 128
MXU_DIM = 256
VMEM_LIMIT_BYTES = 60 * 1024 * 1024

TM = 512
CHUNK = MXU_DIM
CONF_CARRY = 32
CONF_STRIDE = 4
CONF_PARTS = 2


def _rms(x, g):
    ms = jnp.mean(x * x, axis=-1, keepdims=True)
    return x * lax.rsqrt(ms + EPS) * g


def _rms_split(x, g):
    r = lax.rsqrt(jnp.mean(x * x, axis=-1, keepdims=True) + EPS)
    return (x * g).astype(BF16), r


def _dot(a, b):
    return jnp.dot(a, b, preferred_element_type=F32)


def _dot_nt(a, b):
    return lax.dot_general(a, b, (((1,), (1,)), ((), ())), preferred_element_type=F32)


def _sigmoid(x):
    return 1.0 / (1.0 + jnp.exp(-x))


def _shifted(prev, cur, shift):
    n_prev = prev.shape[0]
    ext = jnp.concatenate([prev, cur], axis=0)
    return pltpu.roll(ext, shift, 0)[n_prev:]


def _to_bf16(src_ref, dst_ref, rows=256):
    for r in range(0, src_ref.shape[0], rows):
        dst_ref[r:r + rows, :] = src_ref[r:r + rows, :].astype(BF16)


def _resident(shape):
    nd = len(shape)
    return pl.BlockSpec(shape, lambda *_: (0,) * nd, pipeline_mode=pl.Buffered(1))


def _layer(shape, layer):
    nd = len(shape) - 1
    return pl.BlockSpec((None,) + tuple(shape[1:]), lambda *_: (layer,) + (0,) * nd,
                        pipeline_mode=pl.Buffered(1))


def _params():
    return pltpu.CompilerParams(dimension_semantics=("arbitrary",),
                                vmem_limit_bytes=VMEM_LIMIT_BYTES)


def _sconv_kernel(x_ref, g_ref, win_f32, cw_ref, wout_f32, o_ref, carry_ref, z_ref,
                  win_ref, wout_ref, *, tiles_per_seq):
    tm = x_ref.shape[0]

    @pl.when(pl.program_id(0) == 0)
    def _():
        _to_bf16(win_f32, win_ref)
        _to_bf16(wout_f32, wout_ref)

    first = (pl.program_id(0) % tiles_per_seq) == 0
    x = x_ref[...]
    xg, r = _rms_split(x, g_ref[...])
    cw = cw_ref[...]
    for j in range(D_MODEL // CHUNK):
        lo, hi = j * CHUNK, (j + 1) * CHUNK
        b = _dot(xg, win_ref[:, lo:hi]) * r
        c = _dot(xg, win_ref[:, D_MODEL + lo:D_MODEL + hi]) * r
        v = _dot(xg, win_ref[:, 2 * D_MODEL + lo:2 * D_MODEL + hi]) * r
        u = c * v
        prev = jnp.where(first, 0.0, carry_ref[:, lo:hi])
        carry_ref[:, lo:hi] = u[tm - SUBLANES:, :]
        conv = (cw[2:3, lo:hi] * u
                + cw[1:2, lo:hi] * _shifted(prev, u, 1)
                + cw[0:1, lo:hi] * _shifted(prev, u, 2))
        z_ref[:, lo:hi] = (b * conv).astype(BF16)
    o_ref[...] = x + _dot(z_ref[...], wout_ref[...])


def _sconv_mixer(x, g, w_in, conv_w, w_out, *, layer, j, seq):
    n, d = x.shape
    kern = functools.partial(_sconv_kernel, tiles_per_seq=seq // TM)
    return pl.pallas_call(
        kern,
        out_shape=jax.ShapeDtypeStruct((n, d), F32),
        grid=(n // TM,),
        in_specs=[
            pl.BlockSpec((TM, d), lambda i: (i, 0)),
            _layer(g.shape, layer),
            _layer(w_in.shape, j),
            _layer(conv_w.shape, j),
            _layer(w_out.shape, j),
        ],
        out_specs=pl.BlockSpec((TM, d), lambda i: (i, 0)),
        scratch_shapes=[pltpu.VMEM((SUBLANES, d), F32),
                        pltpu.VMEM((TM, d), BF16),
                        pltpu.VMEM(w_in.shape[1:], BF16),
                        pltpu.VMEM(w_out.shape[1:], BF16)],
        compiler_params=_params(),
        name="sconv_mixer",
    )(x, g, w_in, conv_w, w_out)


def _swa_kernel(sinks_ref, x_ref, g_ref, wqkv_f32, bqkv_ref, wo_f32, bo_ref, o_ref,
                q_s, k2_s, v2_s, oh_s, bias_s, wqkv_ref, wo_ref, *, tiles_per_seq):
    tm = x_ref.shape[0]
    n_blocks = tm // BLOCK
    group = ATTN_HEADS // ATTN_KV_HEADS
    q_dim = ATTN_HEADS * HEAD_DIM
    kv_dim = ATTN_KV_HEADS * HEAD_DIM
    first = (pl.program_id(0) % tiles_per_seq) == 0

    @pl.when(first)
    def _():
        k2_s[:, :, 0:BLOCK, :] = jnp.zeros((ATTN_KV_HEADS, 2, BLOCK, LANES), BF16)
        v2_s[:, :, 0:BLOCK, :] = jnp.zeros((ATTN_KV_HEADS, 2, BLOCK, LANES), BF16)

    @pl.when(pl.program_id(0) == 0)
    def _():
        _to_bf16(wqkv_f32, wqkv_ref)
        _to_bf16(wo_f32, wo_ref)
        qi = lax.broadcasted_iota(jnp.int32, (BLOCK, 2 * BLOCK), 0)
        kj = lax.broadcasted_iota(jnp.int32, (BLOCK, 2 * BLOCK), 1)
        dist = BLOCK + qi - kj
        neg_inf = jnp.float32(-jnp.inf)
        pen = jnp.where((dist >= 0) & (dist < WINDOW), 0.0, neg_inf)
        pen_start = jnp.where(kj < BLOCK, neg_inf, pen)
        distf = dist.astype(F32)
        for h in range(ATTN_HEADS):
            slope = 2.0 ** (-8.0 * (h + 1) / ATTN_HEADS)
            bias_s[0, h] = pen - slope * distf
            bias_s[1, h] = pen_start - slope * distf

    x = x_ref[...]
    xg, r = _rms_split(x, g_ref[...])
    scale = HEAD_DIM ** -0.5
    for j in range(q_dim // CHUNK):
        lo, hi = j * CHUNK, (j + 1) * CHUNK
        q_s[:, lo:hi] = ((_dot(xg, wqkv_ref[:, lo:hi]) * r + bqkv_ref[:, lo:hi]) * scale).astype(BF16)

    lane = lax.broadcasted_iota(jnp.int32, (tm, LANES), 1)
    for (dst, base) in ((k2_s, q_dim), (v2_s, q_dim + kv_dim)):
        for gi in range(kv_dim // LANES):
            lo = base + gi * LANES
            kg = _dot(xg, wqkv_ref[:, lo:lo + LANES]) * r + bqkv_ref[:, lo:lo + LANES]
            low = jnp.where(lane < HEAD_DIM, kg, 0.0)
            high = jnp.where(lane >= HEAD_DIM, kg, 0.0)
            low_hi = pltpu.roll(low, HEAD_DIM, 1)
            high_lo = pltpu.roll(high, HEAD_DIM, 1)
            dst[2 * gi, 0, BLOCK:, :] = low.astype(BF16)
            dst[2 * gi, 1, BLOCK:, :] = low_hi.astype(BF16)
            dst[2 * gi + 1, 0, BLOCK:, :] = high_lo.astype(BF16)
            dst[2 * gi + 1, 1, BLOCK:, :] = high.astype(BF16)

    for n in range(n_blocks):
        row = n * BLOCK
        variant = jnp.where(first, 1, 0) if n == 0 else 0
        for kh in range(ATTN_KV_HEADS):
            heads = range(kh * group, (kh + 1) * group)
            scores = []
            for h in heads:
                qp = q_s[pl.ds(row, BLOCK), (h // 2) * LANES:(h // 2 + 1) * LANES]
                kb = k2_s[kh, h % 2, pl.ds(row, 2 * BLOCK), :]
                scores.append(_dot_nt(qp, kb))
            outs = []
            for h, s in zip(heads, scores):
                s = s + bias_s[variant, h]
                sink = sinks_ref[h]
                m = jnp.maximum(jnp.max(s, axis=-1, keepdims=True), sink)
                e = jnp.exp(s - m)
                denom = jnp.sum(e, axis=-1, keepdims=True) + jnp.exp(sink - m)
                vb = v2_s[kh, h % 2, pl.ds(row, 2 * BLOCK), :]
                outs.append(_dot(e.astype(BF16), vb) * (1.0 / denom))
            for i in range(group // 2):
                p = (kh * group) // 2 + i
                o_pair = outs[2 * i] + outs[2 * i + 1]
                oh_s[pl.ds(row, BLOCK), p * LANES:(p + 1) * LANES] = o_pair.astype(BF16)

    k2_s[:, :, 0:BLOCK, :] = k2_s[:, :, tm:tm + BLOCK, :]
    v2_s[:, :, 0:BLOCK, :] = v2_s[:, :, tm:tm + BLOCK, :]

    o_ref[...] = x + _dot(oh_s[...], wo_ref[...]) + bo_ref[...]


def _swa_mixer(x, g, w_qkv, b_qkv, sinks, w_o, b_o, *, layer, j, seq):
    n, d = x.shape
    kern = functools.partial(_swa_kernel, tiles_per_seq=seq // TM)
    q_dim = ATTN_HEADS * HEAD_DIM
    return pl.pallas_call(
        kern,
        out_shape=jax.ShapeDtypeStruct((n, d), F32),
        grid=(n // TM,),
        in_specs=[
            pl.BlockSpec(memory_space=pltpu.SMEM),
            pl.BlockSpec((TM, d), lambda i: (i, 0)),
            _layer(g.shape, layer),
            _layer(w_qkv.shape, j),
            _layer(b_qkv.shape, j),
            _layer(w_o.shape, j),
            _layer(b_o.shape, j),
        ],
        out_specs=pl.BlockSpec((TM, d), lambda i: (i, 0)),
        scratch_shapes=[
            pltpu.VMEM((TM, q_dim), BF16),
            pltpu.VMEM((ATTN_KV_HEADS, 2, BLOCK + TM, LANES), BF16),
            pltpu.VMEM((ATTN_KV_HEADS, 2, BLOCK + TM, LANES), BF16),
            pltpu.VMEM((TM, q_dim), BF16),
            pltpu.VMEM((2, ATTN_HEADS, BLOCK, 2 * BLOCK), F32),
            pltpu.VMEM(w_qkv.shape[1:], BF16),
            pltpu.VMEM(w_o.shape[1:], BF16),
        ],
        compiler_params=_params(),
        name="swa_mixer",
    )(sinks, x, g, w_qkv, b_qkv, w_o, b_o)


def _conformer_kernel(x_ref, g_ref, w1_f32, b1_ref, wdw_ref, bdw_ref, lng_ref, lnb_ref,
                      w2_f32, b2_ref, o_ref, ext_s, u_s, w1_ref, w2_ref, *, tiles_per_seq):
    tm = x_ref.shape[0]
    n_slabs = D_MODEL // LANES
    rows = tm // CONF_STRIDE

    @pl.when(pl.program_id(0) == 0)
    def _():
        _to_bf16(w1_f32, w1_ref)
        _to_bf16(w2_f32, w2_ref)

    first = (pl.program_id(0) % tiles_per_seq) == 0
    x = x_ref[...]
    xg, r = _rms_split(x, g_ref[...])
    for j in range(D_MODEL // CHUNK):
        lo, hi = j * CHUNK, (j + 1) * CHUNK
        a = _dot(xg, w1_ref[:, lo:hi]) * r + b1_ref[:, lo:hi]
        gate = _dot(xg, w1_ref[:, D_MODEL + lo:D_MODEL + hi]) * r + b1_ref[:, D_MODEL + lo:D_MODEL + hi]
        u = a * _sigmoid(gate)
        for i in range(CHUNK // LANES):
            k = j * (CHUNK // LANES) + i
            klo, khi = k * LANES, (k + 1) * LANES
            ext_s[k, 0:CONF_CARRY, :] = jnp.where(first, 0.0, ext_s[k, tm:tm + CONF_CARRY, :])
            ext_s[k, CONF_CARRY:CONF_CARRY + tm, :] = u[:, i * LANES:(i + 1) * LANES]
            for part in range(CONF_PARTS):
                n = rows // CONF_PARTS
                base = part * n * CONF_STRIDE
                acc = [jnp.zeros((n, LANES), F32) + bdw_ref[:, klo:khi] for _ in range(CONF_STRIDE)]
                for d in range(CONF_STRIDE - 1, -CONF_WIDTH, -1):
                    src = ext_s[k, pl.ds(CONF_CARRY + base + d, n, stride=CONF_STRIDE), :]
                    for p in range(CONF_STRIDE):
                        delay = p - d
                        if 0 <= delay < CONF_WIDTH:
                            tap = CONF_WIDTH - 1 - delay
                            acc[p] = acc[p] + wdw_ref[tap:tap + 1, klo:khi] * src
                for p in range(CONF_STRIDE):
                    u_s[k, pl.ds(base + p, n, stride=CONF_STRIDE), :] = acc[p]
    u = jnp.concatenate([u_s[k] for k in range(n_slabs)], axis=1)
    mu = jnp.mean(u, axis=-1, keepdims=True)
    uc = u - mu
    y = uc * lax.rsqrt(jnp.mean(uc * uc, axis=-1, keepdims=True) + EPS)
    y = y * lng_ref[...] + lnb_ref[...]
    y = (y * _sigmoid(y)).astype(BF16)
    o_ref[...] = x + _dot(y, w2_ref[...]) + b2_ref[...]


def _conformer_mixer(x, g, w1, b1, wdw, bdw, lng, lnb, w2, b2, *, layer, j, seq):
    n, d = x.shape
    kern = functools.partial(_conformer_kernel, tiles_per_seq=seq // TM)
    return pl.pallas_call(
        kern,
        out_shape=jax.ShapeDtypeStruct((n, d), F32),
        grid=(n // TM,),
        in_specs=[pl.BlockSpec((TM, d), lambda i: (i, 0))]
        + [_layer(g.shape, layer)]
        + [_layer(a.shape, j) for a in (w1, b1, wdw, bdw, lng, lnb, w2, b2)],
        out_specs=pl.BlockSpec((TM, d), lambda i: (i, 0)),
        scratch_shapes=[pltpu.VMEM((d // LANES, CONF_CARRY + TM, LANES), F32),
                        pltpu.VMEM((d // LANES, TM, LANES), F32),
                        pltpu.VMEM(w1.shape[1:], BF16),
                        pltpu.VMEM(w2.shape[1:], BF16)],
        compiler_params=_params(),
        name="conformer_mixer",
    )(x, g, w1, b1, wdw, bdw, lng, lnb, w2, b2)


def _memfold_kernel(mem_ref, g_ref, wkv_ref, wq_ref, wo_ref, m_ref, vw_ref):
    bsz, n_mem, d = mem_ref.shape
    hd = XATTN_HEAD_DIM
    mn = _rms(mem_ref[...].reshape(bsz * n_mem, d), g_ref[...]).astype(BF16)
    k = _dot(mn, wkv_ref[:, :d].astype(BF16)).astype(BF16)
    v = _dot(mn, wkv_ref[:, d:].astype(BF16)).astype(BF16)
    scale = hd ** -0.5
    for h in range(XATTN_HEADS):
        lo, hi = h * hd, (h + 1) * hd
        wq = wq_ref[:, lo:hi].astype(BF16)
        wo = wo_ref[lo:hi, :].astype(BF16)
        for b in range(bsz):
            r0, r1 = b * n_mem, (b + 1) * n_mem
            m_ref[b, :, lo:hi] = (_dot_nt(wq, k[r0:r1, lo:hi]) * scale).astype(BF16)
            vw_ref[b, lo:hi, :] = _dot(v[r0:r1, lo:hi], wo).astype(BF16)


def _memory_fold(mem, g_mem, w_kv, w_q, w_o):
    bsz, m, d = mem.shape
    depth = w_kv.shape[0]
    hm = XATTN_HEADS * m
    return pl.pallas_call(
        _memfold_kernel,
        out_shape=(jax.ShapeDtypeStruct((depth, bsz, d, hm), BF16),
                   jax.ShapeDtypeStruct((depth, bsz, hm, d), BF16)),
        grid=(depth,),
        in_specs=[
            _resident(mem.shape),
            pl.BlockSpec((None, 1, d), lambda l: (l, 0, 0)),
            pl.BlockSpec((None, d, 2 * d), lambda l: (l, 0, 0)),
            pl.BlockSpec((None, d, d), lambda l: (l, 0, 0)),
            pl.BlockSpec((None, d, d), lambda l: (l, 0, 0)),
        ],
        out_specs=(pl.BlockSpec((None, bsz, d, hm), lambda l: (l, 0, 0, 0)),
                   pl.BlockSpec((None, bsz, hm, d), lambda l: (l, 0, 0, 0))),
        compiler_params=_params(),
        name="memory_fold",
    )(mem, g_mem, w_kv, w_q, w_o)


def _xattn_tile(x, g_ref, m_ref, vw_ref, p_s):
    n_mem = m_ref.shape[-1] // XATTN_HEADS
    xg, r = _rms_split(x, g_ref[...])
    scores = [_dot(xg, m_ref[:, h * n_mem:(h + 1) * n_mem]) for h in range(XATTN_HEADS)]
    for h, s in enumerate(scores):
        s = s * r
        e = jnp.exp(s - jnp.max(s, axis=-1, keepdims=True))
        inv = 1.0 / jnp.sum(e, axis=-1, keepdims=True)
        p_s[:, h * n_mem:(h + 1) * n_mem] = (e * inv).astype(BF16)
    return x + _dot(p_s[...], vw_ref[...])


FFN_PREP = D_FF // CHUNK
UP_SLAB = 2 * D_FF // FFN_PREP


def _up_cols(ref, col):
    slab, off = divmod(col, UP_SLAB)
    return ref[slab, :, off:off + CHUNK]


def _xattn_ffn_kernel(x_ref, gx_ref, m_ref, vw_ref, g_ref, wup_slab, cw_ref, wd_slab, gf_ref,
                      o_ref, p_s, act_s, carry_s, wup_s, wd_s, *, tiles_per_seq, final_norm):
    tm = x_ref.shape[0]
    step = pl.program_id(0)

    @pl.when(step < FFN_PREP)
    def _():
        wup_s[step] = wup_slab[...].astype(BF16)
        row = pl.multiple_of(step * CHUNK, CHUNK)
        wd_s[pl.ds(row, CHUNK), :] = wd_slab[...].astype(BF16)

    @pl.when(step >= FFN_PREP)
    def _():
        first = ((step - FFN_PREP) % tiles_per_seq) == 0
        x = _xattn_tile(x_ref[...], gx_ref, m_ref, vw_ref, p_s)
        xg, r = _rms_split(x, g_ref[...])
        for j in range(D_FF // CHUNK):
            lo, hi = j * CHUNK, (j + 1) * CHUNK
            gate = _dot(xg, _up_cols(wup_s, lo)) * r
            up = _dot(xg, _up_cols(wup_s, D_FF + lo)) * r
            prev = jnp.where(first, 0.0, carry_s[:, lo:hi])
            carry_s[:, lo:hi] = gate[tm - SUBLANES:, :]
            cv = (cw_ref[2:3, lo:hi] * gate
                  + cw_ref[1:2, lo:hi] * _shifted(prev, gate, 1)
                  + cw_ref[0:1, lo:hi] * _shifted(prev, gate, 2))
            act_s[:, lo:hi] = (cv * _sigmoid(cv) * up).astype(BF16)
        y = x + _dot(act_s[...], wd_s[...])
        if final_norm:
            y = _rms(y, gf_ref[...])
        o_ref[...] = y


def _xattn_mlp(x, gx, m_all, vw_all, g, w_up, cw, wd, g_final, *, layer, seq, final_norm):
    n, d = x.shape
    hm = m_all.shape[-1]
    tiles_per_seq = seq // TM
    assert UP_SLAB % CHUNK == 0 and D_FF % UP_SLAB == CHUNK % UP_SLAB
    kern = functools.partial(_xattn_ffn_kernel, tiles_per_seq=tiles_per_seq, final_norm=final_norm)
    last = FFN_PREP - 1
    tile = lambda i: (jnp.maximum(i - FFN_PREP, 0), 0)
    batch = lambda i: (layer, jnp.maximum(i - FFN_PREP, 0) // tiles_per_seq, 0, 0)
    return pl.pallas_call(
        kern,
        out_shape=jax.ShapeDtypeStruct((n, d), F32),
        grid=(FFN_PREP + n // TM,),
        in_specs=[
            pl.BlockSpec((TM, d), tile),
            _layer(gx.shape, layer),
            pl.BlockSpec((None, None, d, hm), batch),
            pl.BlockSpec((None, None, hm, d), batch),
            _layer(g.shape, layer),
            pl.BlockSpec((None, d, UP_SLAB), lambda i: (layer, 0, jnp.minimum(i, last))),
            _layer(cw.shape, layer),
            pl.BlockSpec((None, CHUNK, d), lambda i: (layer, jnp.minimum(i, last), 0)),
            _resident(g_final.shape),
        ],
        out_specs=pl.BlockSpec((TM, d), tile),
        scratch_shapes=[pltpu.VMEM((TM, hm), BF16),
                        pltpu.VMEM((TM, D_FF), BF16),
                        pltpu.VMEM((SUBLANES, D_FF), F32),
                        pltpu.VMEM((FFN_PREP, d, UP_SLAB), BF16),
                        pltpu.VMEM((D_FF, d), BF16)],
        compiler_params=_params(),
        name="xattn_mlp",
    )(x, gx, m_all, vw_all, g, w_up, cw, wd, g_final)


def kernel(x, mem, norm_mix, norm_xattn, norm_mem, norm_ffn, norm_final, a_w_in, a_conv, a_w_out, b_w_qkv, b_b_qkv, b_sinks, b_w_o, b_b_o, c_w_pw1, c_b_pw1, c_w_dw, c_b_dw, c_ln_g, c_ln_b, c_w_pw2, c_b_pw2, x_w_q, x_w_kv, x_w_o, f_w_up, f_conv, f_w_down):
    bsz, seq, d = x.shape
    depth = norm_mix.shape[0]
    assert d == D_MODEL and seq % TM == 0 and TM % BLOCK == 0
    row = lambda p: p.reshape(p.shape[0], 1, p.shape[1])

    h = x.reshape(bsz * seq, d)
    m_all, vw_all = _memory_fold(mem, row(norm_mem), x_w_kv, x_w_q, x_w_o)
    g_mix, g_xattn, g_ffn = row(norm_mix), row(norm_xattn), row(norm_ffn)

    for i in range(depth):
        kind, j = i % 3, i // 3
        if kind == 0:
            h = _sconv_mixer(h, g_mix, a_w_in, a_conv, a_w_out, layer=i, j=j, seq=seq)
        elif kind == 1:
            h = _swa_mixer(h, g_mix, b_w_qkv, row(b_b_qkv), b_sinks[j], b_w_o, row(b_b_o),
                           layer=i, j=j, seq=seq)
        else:
            h = _conformer_mixer(h, g_mix, c_w_pw1, row(c_b_pw1), c_w_dw, row(c_b_dw),
                                 row(c_ln_g), row(c_ln_b), c_w_pw2, row(c_b_pw2),
                                 layer=i, j=j, seq=seq)
        h = _xattn_mlp(h, g_xattn, m_all, vw_all, g_ffn, f_w_up, f_conv, f_w_down,
                       norm_final.reshape(1, d), layer=i, seq=seq, final_norm=(i == depth - 1))
    return h.reshape(bsz, seq, d)
```

```python
import functools

import jax
import jax.numpy as jnp
from jax import lax
from jax.experimental import pallas as pl
from jax.experimental.pallas import tpu as pltpu

EPS = 1e-6
F32 = jnp.float32
BF16 = jnp.bfloat16

D_MODEL = 1024
SC_WIDTH = 3
ATTN_HEADS = 16
ATTN_KV_HEADS = 4
HEAD_DIM = 64
WINDOW = 128
BLOCK = 128
CONF_WIDTH = 31
XATTN_HEADS = 4
XATTN_HEAD_DIM = D_MODEL // XATTN_HEADS
D_FF = 2816
FFN_CONV_WIDTH = 3

SUBLANES = 8
LANES = 128
MXU_DIM = 256
VMEM_LIMIT_BYTES = 60 * 1024 * 1024

TM = 512
TM_SCONV = 1024
CHUNK = MXU_DIM
CONF_CARRY = 32
CONF_STRIDE = 4
CONF_PARTS = 2


def _rms(x, g):
    ms = jnp.mean(x * x, axis=-1, keepdims=True)
    return x * lax.rsqrt(ms + EPS) * g


def _rms_split(x, g):
    r = lax.rsqrt(jnp.mean(x * x, axis=-1, keepdims=True) + EPS)
    return (x * g).astype(BF16), r


def _dot(a, b):
    return jnp.dot(a, b, preferred_element_type=F32)


def _dot_nt(a, b):
    return lax.dot_general(a, b, (((1,), (1,)), ((), ())), preferred_element_type=F32)


def _sigmoid(x):
    return 1.0 / (1.0 + jnp.exp(-x))


def _shifted(prev, cur, shift):
    n_prev = prev.shape[0]
    ext = jnp.concatenate([prev, cur], axis=0)
    return pltpu.roll(ext, shift, 0)[n_prev:]


def _to_bf16(src_ref, dst_ref, rows=256):
    for r in range(0, src_ref.shape[0], rows):
        dst_ref[r:r + rows, :] = src_ref[r:r + rows, :].astype(BF16)


def _resident(shape):
    nd = len(shape)
    return pl.BlockSpec(shape, lambda *_: (0,) * nd, pipeline_mode=pl.Buffered(1))


def _layer(shape, layer):
    nd = len(shape) - 1
    return pl.BlockSpec((None,) + tuple(shape[1:]), lambda *_: (layer,) + (0,) * nd,
                        pipeline_mode=pl.Buffered(1))


def _params():
    return pltpu.CompilerParams(dimension_semantics=("arbitrary",),
                                vmem_limit_bytes=VMEM_LIMIT_BYTES)


def _sconv_kernel(x_ref, g_ref, win_f32, cw_ref, wout_f32, o_ref, carry_ref, z_ref,
                  win_ref, wout_ref, *, tiles_per_seq):
    tm = x_ref.shape[0]

    @pl.when(pl.program_id(0) == 0)
    def _():
        _to_bf16(win_f32, win_ref)
        _to_bf16(wout_f32, wout_ref)

    first = (pl.program_id(0) % tiles_per_seq) == 0
    x = x_ref[...]
    xg, r = _rms_split(x, g_ref[...])
    cw = cw_ref[...]
    for j in range(D_MODEL // CHUNK):
        lo, hi = j * CHUNK, (j + 1) * CHUNK
        b = _dot(xg, win_ref[:, lo:hi]) * r
        c = _dot(xg, win_ref[:, D_MODEL + lo:D_MODEL + hi]) * r
        v = _dot(xg, win_ref[:, 2 * D_MODEL + lo:2 * D_MODEL + hi]) * r
        u = c * v
        prev = jnp.where(first, 0.0, carry_ref[:, lo:hi])
        carry_ref[:, lo:hi] = u[tm - SUBLANES:, :]
        conv = (cw[2:3, lo:hi] * u
                + cw[1:2, lo:hi] * _shifted(prev, u, 1)
                + cw[0:1, lo:hi] * _shifted(prev, u, 2))
        z_ref[:, lo:hi] = (b * conv).astype(BF16)
    o_ref[...] = x + _dot(z_ref[...], wout_ref[...])


def _sconv_mixer(x, g, w_in, conv_w, w_out, *, layer, j, seq):
    n, d = x.shape
    tm = TM_SCONV
    assert seq % tm == 0
    kern = functools.partial(_sconv_kernel, tiles_per_seq=seq // tm)
    return pl.pallas_call(
        kern,
        out_shape=jax.ShapeDtypeStruct((n, d), F32),
        grid=(n // tm,),
        in_specs=[
            pl.BlockSpec((tm, d), lambda i: (i, 0)),
            _layer(g.shape, layer),
            _layer(w_in.shape, j),
            _layer(conv_w.shape, j),
            _layer(w_out.shape, j),
        ],
        out_specs=pl.BlockSpec((tm, d), lambda i: (i, 0)),
        scratch_shapes=[pltpu.VMEM((SUBLANES, d), F32),
                        pltpu.VMEM((tm, d), BF16),
                        pltpu.VMEM(w_in.shape[1:], BF16),
                        pltpu.VMEM(w_out.shape[1:], BF16)],
        compiler_params=_params(),
        name="sconv_mixer",
    )(x, g, w_in, conv_w, w_out)


def _swa_kernel(sinks_ref, x_ref, g_ref, wqkv_f32, bqkv_ref, wo_f32, bo_ref, o_ref,
                q_s, k2_s, v2_s, oh_s, bias_s, wqkv_ref, wo_ref, *, tiles_per_seq):
    tm = x_ref.shape[0]
    n_blocks = tm // BLOCK
    group = ATTN_HEADS // ATTN_KV_HEADS
    q_dim = ATTN_HEADS * HEAD_DIM
    kv_dim = ATTN_KV_HEADS * HEAD_DIM
    first = (pl.program_id(0) % tiles_per_seq) == 0

    @pl.when(first)
    def _():
        k2_s[:, :, 0:BLOCK, :] = jnp.zeros((ATTN_KV_HEADS, 2, BLOCK, LANES), BF16)
        v2_s[:, :, 0:BLOCK, :] = jnp.zeros((ATTN_KV_HEADS, 2, BLOCK, LANES), BF16)

    @pl.when(pl.program_id(0) == 0)
    def _():
        _to_bf16(wqkv_f32, wqkv_ref)
        _to_bf16(wo_f32, wo_ref)
        qi = lax.broadcasted_iota(jnp.int32, (BLOCK, 2 * BLOCK), 0)
        kj = lax.broadcasted_iota(jnp.int32, (BLOCK, 2 * BLOCK), 1)
        dist = BLOCK + qi - kj
        neg_inf = jnp.float32(-jnp.inf)
        pen = jnp.where((dist >= 0) & (dist < WINDOW), 0.0, neg_inf)
        pen_start = jnp.where(kj < BLOCK, neg_inf, pen)
        distf = dist.astype(F32)
        for h in range(ATTN_HEADS):
            slope = 2.0 ** (-8.0 * (h + 1) / ATTN_HEADS)
            bias_s[0, h] = pen - slope * distf
            bias_s[1, h] = pen_start - slope * distf

    x = x_ref[...]
    xg, r = _rms_split(x, g_ref[...])
    scale = HEAD_DIM ** -0.5
    for j in range(q_dim // CHUNK):
        lo, hi = j * CHUNK, (j + 1) * CHUNK
        q_s[:, lo:hi] = ((_dot(xg, wqkv_ref[:, lo:hi]) * r + bqkv_ref[:, lo:hi]) * scale).astype(BF16)

    lane = lax.broadcasted_iota(jnp.int32, (tm, LANES), 1)
    for (dst, base) in ((k2_s, q_dim), (v2_s, q_dim + kv_dim)):
        for gi in range(kv_dim // LANES):
            lo = base + gi * LANES
            kg = _dot(xg, wqkv_ref[:, lo:lo + LANES]) * r + bqkv_ref[:, lo:lo + LANES]
            low = jnp.where(lane < HEAD_DIM, kg, 0.0)
            high = jnp.where(lane >= HEAD_DIM, kg, 0.0)
            low_hi = pltpu.roll(low, HEAD_DIM, 1)
            high_lo = pltpu.roll(high, HEAD_DIM, 1)
            dst[2 * gi, 0, BLOCK:, :] = low.astype(BF16)
            dst[2 * gi, 1, BLOCK:, :] = low_hi.astype(BF16)
            dst[2 * gi + 1, 0, BLOCK:, :] = high_lo.astype(BF16)
            dst[2 * gi + 1, 1, BLOCK:, :] = high.astype(BF16)

    for n in range(n_blocks):
        row = n * BLOCK
        variant = jnp.where(first, 1, 0) if n == 0 else 0
        for kh in range(ATTN_KV_HEADS):
            heads = range(kh * group, (kh + 1) * group)
            scores = []
            for h in heads:
                qp = q_s[pl.ds(row, BLOCK), (h // 2) * LANES:(h // 2 + 1) * LANES]
                kb = k2_s[kh, h % 2, pl.ds(row, 2 * BLOCK), :]
                scores.append(_dot_nt(qp, kb))
            outs = []
            for h, s in zip(heads, scores):
                s = s + bias_s[variant, h]
                sink = sinks_ref[h]
                m = jnp.maximum(jnp.max(s, axis=-1, keepdims=True), sink)
                e = jnp.exp(s - m)
                denom = jnp.sum(e, axis=-1, keepdims=True) + jnp.exp(sink - m)
                vb = v2_s[kh, h % 2, pl.ds(row, 2 * BLOCK), :]
                outs.append(_dot(e.astype(BF16), vb) * (1.0 / denom))
            for i in range(group // 2):
                p = (kh * group) // 2 + i
                o_pair = outs[2 * i] + outs[2 * i + 1]
                oh_s[pl.ds(row, BLOCK), p * LANES:(p + 1) * LANES] = o_pair.astype(BF16)

    k2_s[:, :, 0:BLOCK, :] = k2_s[:, :, tm:tm + BLOCK, :]
    v2_s[:, :, 0:BLOCK, :] = v2_s[:, :, tm:tm + BLOCK, :]

    o_ref[...] = x + _dot(oh_s[...], wo_ref[...]) + bo_ref[...]


def _swa_mixer(x, g, w_qkv, b_qkv, sinks, w_o, b_o, *, layer, j, seq):
    n, d = x.shape
    kern = functools.partial(_swa_kernel, tiles_per_seq=seq // TM)
    q_dim = ATTN_HEADS * HEAD_DIM
    return pl.pallas_call(
        kern,
        out_shape=jax.ShapeDtypeStruct((n, d), F32),
        grid=(n // TM,),
        in_specs=[
            pl.BlockSpec(memory_space=pltpu.SMEM),
            pl.BlockSpec((TM, d), lambda i: (i, 0)),
            _layer(g.shape, layer),
            _layer(w_qkv.shape, j),
            _layer(b_qkv.shape, j),
            _layer(w_o.shape, j),
            _layer(b_o.shape, j),
        ],
        out_specs=pl.BlockSpec((TM, d), lambda i: (i, 0)),
        scratch_shapes=[
            pltpu.VMEM((TM, q_dim), BF16),
            pltpu.VMEM((ATTN_KV_HEADS, 2, BLOCK + TM, LANES), BF16),
            pltpu.VMEM((ATTN_KV_HEADS, 2, BLOCK + TM, LANES), BF16),
            pltpu.VMEM((TM, q_dim), BF16),
            pltpu.VMEM((2, ATTN_HEADS, BLOCK, 2 * BLOCK), F32),
            pltpu.VMEM(w_qkv.shape[1:], BF16),
            pltpu.VMEM(w_o.shape[1:], BF16),
        ],
        compiler_params=_params(),
        name="swa_mixer",
    )(sinks, x, g, w_qkv, b_qkv, w_o, b_o)


def _conformer_kernel(x_ref, g_ref, w1_f32, b1_ref, wdw_ref, bdw_ref, lng_ref, lnb_ref,
                      w2_f32, b2_ref, o_ref, ext_s, u_s, w1_ref, w2_ref, *, tiles_per_seq):
    tm = x_ref.shape[0]
    n_slabs = D_MODEL // LANES
    rows = tm // CONF_STRIDE

    @pl.when(pl.program_id(0) == 0)
    def _():
        _to_bf16(w1_f32, w1_ref)
        _to_bf16(w2_f32, w2_ref)

    first = (pl.program_id(0) % tiles_per_seq) == 0
    x = x_ref[...]
    xg, r = _rms_split(x, g_ref[...])
    for j in range(D_MODEL // CHUNK):
        lo, hi = j * CHUNK, (j + 1) * CHUNK
        a = _dot(xg, w1_ref[:, lo:hi]) * r + b1_ref[:, lo:hi]
        gate = _dot(xg, w1_ref[:, D_MODEL + lo:D_MODEL + hi]) * r + b1_ref[:, D_MODEL + lo:D_MODEL + hi]
        u = a * _sigmoid(gate)
        for i in range(CHUNK // LANES):
            k = j * (CHUNK // LANES) + i
            klo, khi = k * LANES, (k + 1) * LANES
            ext_s[k, 0:CONF_CARRY, :] = jnp.where(first, 0.0, ext_s[k, tm:tm + CONF_CARRY, :])
            ext_s[k, CONF_CARRY:CONF_CARRY + tm, :] = u[:, i * LANES:(i + 1) * LANES]
            for part in range(CONF_PARTS):
                n = rows // CONF_PARTS
                base = part * n * CONF_STRIDE
                acc = [jnp.zeros((n, LANES), F32) + bdw_ref[:, klo:khi] for _ in range(CONF_STRIDE)]
                for d in range(CONF_STRIDE - 1, -CONF_WIDTH, -1):
                    src = ext_s[k, pl.ds(CONF_CARRY + base + d, n, stride=CONF_STRIDE), :]
                    for p in range(CONF_STRIDE):
                        delay = p - d
                        if 0 <= delay < CONF_WIDTH:
                            tap = CONF_WIDTH - 1 - delay
                            acc[p] = acc[p] + wdw_ref[tap:tap + 1, klo:khi] * src
                for p in range(CONF_STRIDE):
                    u_s[k, pl.ds(base + p, n, stride=CONF_STRIDE), :] = acc[p]
    u = jnp.concatenate([u_s[k] for k in range(n_slabs)], axis=1)
    mu = jnp.mean(u, axis=-1, keepdims=True)
    uc = u - mu
    y = uc * lax.rsqrt(jnp.mean(uc * uc, axis=-1, keepdims=True) + EPS)
    y = y * lng_ref[...] + lnb_ref[...]
    y = (y * _sigmoid(y)).astype(BF16)
    o_ref[...] = x + _dot(y, w2_ref[...]) + b2_ref[...]


def _conformer_mixer(x, g, w1, b1, wdw, bdw, lng, lnb, w2, b2, *, layer, j, seq):
    n, d = x.shape
    kern = functools.partial(_conformer_kernel, tiles_per_seq=seq // TM)
    return pl.pallas_call(
        kern,
        out_shape=jax.ShapeDtypeStruct((n, d), F32),
        grid=(n // TM,),
        in_specs=[pl.BlockSpec((TM, d), lambda i: (i, 0))]
        + [_layer(g.shape, layer)]
        + [_layer(a.shape, j) for a in (w1, b1, wdw, bdw, lng, lnb, w2, b2)],
        out_specs=pl.BlockSpec((TM, d), lambda i: (i, 0)),
        scratch_shapes=[pltpu.VMEM((d // LANES, CONF_CARRY + TM, LANES), F32),
                        pltpu.VMEM((d // LANES, TM, LANES), F32),
                        pltpu.VMEM(w1.shape[1:], BF16),
                        pltpu.VMEM(w2.shape[1:], BF16)],
        compiler_params=_params(),
        name="conformer_mixer",
    )(x, g, w1, b1, wdw, bdw, lng, lnb, w2, b2)


def _memfold_kernel(mem_ref, g_ref, wkv_ref, wq_ref, wo_ref, m_ref, vw_ref):
    bsz, n_mem, d = mem_ref.shape
    hd = XATTN_HEAD_DIM
    mn = _rms(mem_ref[...].reshape(bsz * n_mem, d), g_ref[...]).astype(BF16)
    k = _dot(mn, wkv_ref[:, :d].astype(BF16)).astype(BF16)
    v = _dot(mn, wkv_ref[:, d:].astype(BF16)).astype(BF16)
    scale = hd ** -0.5
    for h in range(XATTN_HEADS):
        lo, hi = h * hd, (h + 1) * hd
        wq = wq_ref[:, lo:hi].astype(BF16)
        wo = wo_ref[lo:hi, :].astype(BF16)
        for b in range(bsz):
            r0, r1 = b * n_mem, (b + 1) * n_mem
            m_ref[b, :, lo:hi] = (_dot_nt(wq, k[r0:r1, lo:hi]) * scale).astype(BF16)
            vw_ref[b, lo:hi, :] = _dot(v[r0:r1, lo:hi], wo).astype(BF16)


def _memory_fold(mem, g_mem, w_kv, w_q, w_o):
    bsz, m, d = mem.shape
    depth = w_kv.shape[0]
    hm = XATTN_HEADS * m
    return pl.pallas_call(
        _memfold_kernel,
        out_shape=(jax.ShapeDtypeStruct((depth, bsz, d, hm), BF16),
                   jax.ShapeDtypeStruct((depth, bsz, hm, d), BF16)),
        grid=(depth,),
        in_specs=[
            _resident(mem.shape),
            pl.BlockSpec((None, 1, d), lambda l: (l, 0, 0)),
            pl.BlockSpec((None, d, 2 * d), lambda l: (l, 0, 0)),
            pl.BlockSpec((None, d, d), lambda l: (l, 0, 0)),
            pl.BlockSpec((None, d, d), lambda l: (l, 0, 0)),
        ],
        out_specs=(pl.BlockSpec((None, bsz, d, hm), lambda l: (l, 0, 0, 0)),
                   pl.BlockSpec((None, bsz, hm, d), lambda l: (l, 0, 0, 0))),
        compiler_params=_params(),
        name="memory_fold",
    )(mem, g_mem, w_kv, w_q, w_o)


def _xattn_tile(x, g_ref, m_ref, vw_ref, p_s):
    n_mem = m_ref.shape[-1] // XATTN_HEADS
    xg, r = _rms_split(x, g_ref[...])
    scores = [_dot(xg, m_ref[:, h * n_mem:(h + 1) * n_mem]) for h in range(XATTN_HEADS)]
    for h, s in enumerate(scores):
        s = s * r
        e = jnp.exp(s - jnp.max(s, axis=-1, keepdims=True))
        inv = 1.0 / jnp.sum(e, axis=-1, keepdims=True)
        p_s[:, h * n_mem:(h + 1) * n_mem] = (e * inv).astype(BF16)
    return x + _dot(p_s[...], vw_ref[...])


FFN_PREP = D_FF // CHUNK
UP_SLAB = 2 * D_FF // FFN_PREP
DOWN_GROUP = 4


def _up_cols(ref, col):
    slab, off = divmod(col, UP_SLAB)
    return ref[slab, :, off:off + CHUNK]


def _xattn_ffn_kernel(x_ref, gx_ref, m_ref, vw_ref, g_ref, wup_slab, cw_ref, wd_slab, gf_ref,
                      o_ref, p_s, act_s, carry_s, wup_s, wd_s, *, tiles_per_seq, final_norm):
    tm = x_ref.shape[0]
    step = pl.program_id(0)

    @pl.when(step < FFN_PREP)
    def _():
        wup_s[step] = wup_slab[...].astype(BF16)
        row = pl.multiple_of(step * CHUNK, CHUNK)
        wd_s[pl.ds(row, CHUNK), :] = wd_slab[...].astype(BF16)

    @pl.when(step >= FFN_PREP)
    def _():
        first = ((step - FFN_PREP) % tiles_per_seq) == 0
        x = _xattn_tile(x_ref[...], gx_ref, m_ref, vw_ref, p_s)
        xg, r = _rms_split(x, g_ref[...])
        y = x
        for j in range(D_FF // CHUNK):
            lo, hi = j * CHUNK, (j + 1) * CHUNK
            gate = _dot(xg, _up_cols(wup_s, lo)) * r
            up = _dot(xg, _up_cols(wup_s, D_FF + lo)) * r
            prev = jnp.where(first, 0.0, carry_s[:, lo:hi])
            carry_s[:, lo:hi] = gate[tm - SUBLANES:, :]
            cv = (cw_ref[2:3, lo:hi] * gate
                  + cw_ref[1:2, lo:hi] * _shifted(prev, gate, 1)
                  + cw_ref[0:1, lo:hi] * _shifted(prev, gate, 2))
            act_s[:, lo:hi] = (cv * _sigmoid(cv) * up).astype(BF16)
            if (j + 1) % DOWN_GROUP == 0 or j + 1 == D_FF // CHUNK:
                k0 = (j // DOWN_GROUP) * DOWN_GROUP * CHUNK
                y = y + _dot(act_s[:, k0:hi], wd_s[k0:hi, :])
        if final_norm:
            y = _rms(y, gf_ref[...])
        o_ref[...] = y


def _xattn_mlp(x, gx, m_all, vw_all, g, w_up, cw, wd, g_final, *, layer, seq, final_norm):
    n, d = x.shape
    hm = m_all.shape[-1]
    tiles_per_seq = seq // TM
    assert UP_SLAB % CHUNK == 0 and D_FF % UP_SLAB == CHUNK % UP_SLAB
    kern = functools.partial(_xattn_ffn_kernel, tiles_per_seq=tiles_per_seq, final_norm=final_norm)
    last = FFN_PREP - 1
    tile = lambda i: (jnp.maximum(i - FFN_PREP, 0), 0)
    batch = lambda i: (layer, jnp.maximum(i - FFN_PREP, 0) // tiles_per_seq, 0, 0)
    return pl.pallas_call(
        kern,
        out_shape=jax.ShapeDtypeStruct((n, d), F32),
        grid=(FFN_PREP + n // TM,),
        in_specs=[
            pl.BlockSpec((TM, d), tile),
            _layer(gx.shape, layer),
            pl.BlockSpec((None, None, d, hm), batch),
            pl.BlockSpec((None, None, hm, d), batch),
            _layer(g.shape, layer),
            pl.BlockSpec((None, d, UP_SLAB), lambda i: (layer, 0, jnp.minimum(i, last))),
            _layer(cw.shape, layer),
            pl.BlockSpec((None, CHUNK, d), lambda i: (layer, jnp.minimum(i, last), 0)),
            _resident(g_final.shape),
        ],
        out_specs=pl.BlockSpec((TM, d), tile),
        scratch_shapes=[pltpu.VMEM((TM, hm), BF16),
                        pltpu.VMEM((TM, D_FF), BF16),
                        pltpu.VMEM((SUBLANES, D_FF), F32),
                        pltpu.VMEM((FFN_PREP, d, UP_SLAB), BF16),
                        pltpu.VMEM((D_FF, d), BF16)],
        compiler_params=_params(),
        name="xattn_mlp",
    )(x, gx, m_all, vw_all, g, w_up, cw, wd, g_final)


def kernel(x, mem, norm_mix, norm_xattn, norm_mem, norm_ffn, norm_final, a_w_in, a_conv, a_w_out, b_w_qkv, b_b_qkv, b_sinks, b_w_o, b_b_o, c_w_pw1, c_b_pw1, c_w_dw, c_b_dw, c_ln_g, c_ln_b, c_w_pw2, c_b_pw2, x_w_q, x_w_kv, x_w_o, f_w_up, f_conv, f_w_down):
    bsz, seq, d = x.shape
    depth = norm_mix.shape[0]
    assert d == D_MODEL and seq % TM == 0 and TM % BLOCK == 0
    row = lambda p: p.reshape(p.shape[0], 1, p.shape[1])

    h = x.reshape(bsz * seq, d)
    m_all, vw_all = _memory_fold(mem, row(norm_mem), x_w_kv, x_w_q, x_w_o)
    g_mix, g_xattn, g_ffn = row(norm_mix), row(norm_xattn), row(norm_ffn)

    for i in range(depth):
        kind, j = i % 3, i // 3
        if kind == 0:
            h = _sconv_mixer(h, g_mix, a_w_in, a_conv, a_w_out, layer=i, j=j, seq=seq)
        elif kind == 1:
            h = _swa_mixer(h, g_mix, b_w_qkv, row(b_b_qkv), b_sinks[j], b_w_o, row(b_b_o),
                           layer=i, j=j, seq=seq)
        else:
            h = _conformer_mixer(h, g_mix, c_w_pw1, row(c_b_pw1), c_w_dw, row(c_b_dw),
                                 row(c_ln_g), row(c_ln_b), c_w_pw2, row(c_b_pw2),
                                 layer=i, j=j, seq=seq)
        h = _xattn_mlp(h, g_xattn, m_all, vw_all, g_ffn, f_w_up, f_conv, f_w_down,
                       norm_final.reshape(1, d), layer=i, seq=seq, final_norm=(i == depth - 1))
    return h.reshape(bsz, seq, d)
```

```python
import functools

import jax
import jax.numpy as jnp
from jax import lax
from jax.experimental import pallas as pl
from jax.experimental.pallas import tpu as pltpu

EPS = 1e-6
F32 = jnp.float32
BF16 = jnp.bfloat16

D_MODEL = 1024
SC_WIDTH = 3
ATTN_HEADS = 16
ATTN_KV_HEADS = 4
HEAD_DIM = 64
WINDOW = 128
BLOCK = 128
CONF_WIDTH = 31
XATTN_HEADS = 4
XATTN_HEAD_DIM = D_MODEL // XATTN_HEADS
D_FF = 2816
FFN_CONV_WIDTH = 3

SUBLANES = 8
LANES = 128
MXU_DIM = 256
VMEM_LIMIT_BYTES = 60 * 1024 * 1024

TM = 512
TM_SCONV = 1024
CHUNK = MXU_DIM
CONF_CARRY = 32
CONF_STRIDE = 4
CONF_PARTS = 2


def _rms(x, g):
    ms = jnp.mean(x * x, axis=-1, keepdims=True)
    return x * lax.rsqrt(ms + EPS) * g


def _rms_split(x, g):
    r = lax.rsqrt(jnp.mean(x * x, axis=-1, keepdims=True) + EPS)
    return (x * g).astype(BF16), r


def _dot(a, b):
    return jnp.dot(a, b, preferred_element_type=F32)


def _dot_nt(a, b):
    return lax.dot_general(a, b, (((1,), (1,)), ((), ())), preferred_element_type=F32)


def _sigmoid(x):
    return 1.0 / (1.0 + jnp.exp(-x))


def _shifted(prev, cur, shift):
    n_prev = prev.shape[0]
    ext = jnp.concatenate([prev, cur], axis=0)
    return pltpu.roll(ext, shift, 0)[n_prev:]


def _to_bf16(src_ref, dst_ref, rows=256):
    for r in range(0, src_ref.shape[0], rows):
        dst_ref[r:r + rows, :] = src_ref[r:r + rows, :].astype(BF16)


def _resident(shape):
    nd = len(shape)
    return pl.BlockSpec(shape, lambda *_: (0,) * nd, pipeline_mode=pl.Buffered(1))


def _layer(shape, layer):
    nd = len(shape) - 1
    return pl.BlockSpec((None,) + tuple(shape[1:]), lambda *_: (layer,) + (0,) * nd,
                        pipeline_mode=pl.Buffered(1))


def _params():
    return pltpu.CompilerParams(dimension_semantics=("arbitrary",),
                                vmem_limit_bytes=VMEM_LIMIT_BYTES)


def _sconv_kernel(x_ref, g_ref, win_f32, cw_ref, wout_f32, o_ref, carry_ref, z_ref,
                  win_ref, wout_ref, *, tiles_per_seq):
    tm = x_ref.shape[0]

    @pl.when(pl.program_id(0) == 0)
    def _():
        _to_bf16(win_f32, win_ref)
        _to_bf16(wout_f32, wout_ref)

    first = (pl.program_id(0) % tiles_per_seq) == 0
    x = x_ref[...]
    xg, r = _rms_split(x, g_ref[...])
    cw = cw_ref[...]
    for j in range(D_MODEL // CHUNK):
        lo, hi = j * CHUNK, (j + 1) * CHUNK
        b = _dot(xg, win_ref[:, lo:hi]) * r
        c = _dot(xg, win_ref[:, D_MODEL + lo:D_MODEL + hi]) * r
        v = _dot(xg, win_ref[:, 2 * D_MODEL + lo:2 * D_MODEL + hi]) * r
        u = c * v
        prev = jnp.where(first, 0.0, carry_ref[:, lo:hi])
        carry_ref[:, lo:hi] = u[tm - SUBLANES:, :]
        conv = (cw[2:3, lo:hi] * u
                + cw[1:2, lo:hi] * _shifted(prev, u, 1)
                + cw[0:1, lo:hi] * _shifted(prev, u, 2))
        z_ref[:, lo:hi] = (b * conv).astype(BF16)
    o_ref[...] = x + _dot(z_ref[...], wout_ref[...])


def _sconv_mixer(x, g, w_in, conv_w, w_out, *, layer, j, seq):
    n, d = x.shape
    tm = TM_SCONV
    assert seq % tm == 0
    kern = functools.partial(_sconv_kernel, tiles_per_seq=seq // tm)
    return pl.pallas_call(
        kern,
        out_shape=jax.ShapeDtypeStruct((n, d), F32),
        grid=(n // tm,),
        in_specs=[
            pl.BlockSpec((tm, d), lambda i: (i, 0)),
            _layer(g.shape, layer),
            _layer(w_in.shape, j),
            _layer(conv_w.shape, j),
            _layer(w_out.shape, j),
        ],
        out_specs=pl.BlockSpec((tm, d), lambda i: (i, 0)),
        scratch_shapes=[pltpu.VMEM((SUBLANES, d), F32),
                        pltpu.VMEM((tm, d), BF16),
                        pltpu.VMEM(w_in.shape[1:], BF16),
                        pltpu.VMEM(w_out.shape[1:], BF16)],
        compiler_params=_params(),
        name="sconv_mixer",
    )(x, g, w_in, conv_w, w_out)


def _swa_kernel(sinks_ref, x_ref, g_ref, wqkv_f32, bqkv_ref, wo_f32, bo_ref, o_ref,
                q_s, k2_s, v2_s, oh_s, bias_s, wqkv_ref, wo_ref, *, tiles_per_seq):
    tm = x_ref.shape[0]
    n_blocks = tm // BLOCK
    group = ATTN_HEADS // ATTN_KV_HEADS
    q_dim = ATTN_HEADS * HEAD_DIM
    kv_dim = ATTN_KV_HEADS * HEAD_DIM
    first = (pl.program_id(0) % tiles_per_seq) == 0

    @pl.when(first)
    def _():
        k2_s[:, :, 0:BLOCK, :] = jnp.zeros((ATTN_KV_HEADS, 2, BLOCK, LANES), BF16)
        v2_s[:, :, 0:BLOCK, :] = jnp.zeros((ATTN_KV_HEADS, 2, BLOCK, LANES), BF16)

    @pl.when(pl.program_id(0) == 0)
    def _():
        _to_bf16(wqkv_f32, wqkv_ref)
        _to_bf16(wo_f32, wo_ref)
        qi = lax.broadcasted_iota(jnp.int32, (BLOCK, 2 * BLOCK), 0)
        kj = lax.broadcasted_iota(jnp.int32, (BLOCK, 2 * BLOCK), 1)
        dist = BLOCK + qi - kj
        neg_inf = jnp.float32(-jnp.inf)
        pen = jnp.where((dist >= 0) & (dist < WINDOW), 0.0, neg_inf)
        pen_start = jnp.where(kj < BLOCK, neg_inf, pen)
        distf = dist.astype(F32)
        for h in range(ATTN_HEADS):
            slope = 2.0 ** (-8.0 * (h + 1) / ATTN_HEADS)
            bias_s[0, h] = pen - slope * distf
            bias_s[1, h] = pen_start - slope * distf

    x = x_ref[...]
    xg, r = _rms_split(x, g_ref[...])
    scale = HEAD_DIM ** -0.5
    for j in range(q_dim // CHUNK):
        lo, hi = j * CHUNK, (j + 1) * CHUNK
        q_s[:, lo:hi] = ((_dot(xg, wqkv_ref[:, lo:hi]) * r + bqkv_ref[:, lo:hi]) * scale).astype(BF16)

    lane = lax.broadcasted_iota(jnp.int32, (tm, LANES), 1)
    for (dst, base) in ((k2_s, q_dim), (v2_s, q_dim + kv_dim)):
        for gi in range(kv_dim // LANES):
            lo = base + gi * LANES
            kg = _dot(xg, wqkv_ref[:, lo:lo + LANES]) * r + bqkv_ref[:, lo:lo + LANES]
            low = jnp.where(lane < HEAD_DIM, kg, 0.0)
            high = jnp.where(lane >= HEAD_DIM, kg, 0.0)
            low_hi = pltpu.roll(low, HEAD_DIM, 1)
            high_lo = pltpu.roll(high, HEAD_DIM, 1)
            dst[2 * gi, 0, BLOCK:, :] = low.astype(BF16)
            dst[2 * gi, 1, BLOCK:, :] = low_hi.astype(BF16)
            dst[2 * gi + 1, 0, BLOCK:, :] = high_lo.astype(BF16)
            dst[2 * gi + 1, 1, BLOCK:, :] = high.astype(BF16)

    for n in range(n_blocks):
        row = n * BLOCK
        variant = jnp.where(first, 1, 0) if n == 0 else 0
        for kh in range(ATTN_KV_HEADS):
            heads = range(kh * group, (kh + 1) * group)
            scores = []
            for h in heads:
                qp = q_s[pl.ds(row, BLOCK), (h // 2) * LANES:(h // 2 + 1) * LANES]
                kb = k2_s[kh, h % 2, pl.ds(row, 2 * BLOCK), :]
                scores.append(_dot_nt(qp, kb))
            outs = []
            for h, s in zip(heads, scores):
                s = s + bias_s[variant, h]
                sink = sinks_ref[h]
                m = jnp.maximum(jnp.max(s, axis=-1, keepdims=True), sink)
                e = jnp.exp(s - m)
                denom = jnp.sum(e, axis=-1, keepdims=True) + jnp.exp(sink - m)
                vb = v2_s[kh, h % 2, pl.ds(row, 2 * BLOCK), :]
                outs.append(_dot(e.astype(BF16), vb) * (1.0 / denom))
            for i in range(group // 2):
                p = (kh * group) // 2 + i
                o_pair = outs[2 * i] + outs[2 * i + 1]
                oh_s[pl.ds(row, BLOCK), p * LANES:(p + 1) * LANES] = o_pair.astype(BF16)

    k2_s[:, :, 0:BLOCK, :] = k2_s[:, :, tm:tm + BLOCK, :]
    v2_s[:, :, 0:BLOCK, :] = v2_s[:, :, tm:tm + BLOCK, :]

    o_ref[...] = x + _dot(oh_s[...], wo_ref[...]) + bo_ref[...]


def _swa_mixer(x, g, w_qkv, b_qkv, sinks, w_o, b_o, *, layer, j, seq):
    n, d = x.shape
    kern = functools.partial(_swa_kernel, tiles_per_seq=seq // TM)
    q_dim = ATTN_HEADS * HEAD_DIM
    return pl.pallas_call(
        kern,
        out_shape=jax.ShapeDtypeStruct((n, d), F32),
        grid=(n // TM,),
        in_specs=[
            pl.BlockSpec(memory_space=pltpu.SMEM),
            pl.BlockSpec((TM, d), lambda i: (i, 0)),
            _layer(g.shape, layer),
            _layer(w_qkv.shape, j),
            _layer(b_qkv.shape, j),
            _layer(w_o.shape, j),
            _layer(b_o.shape, j),
        ],
        out_specs=pl.BlockSpec((TM, d), lambda i: (i, 0)),
        scratch_shapes=[
            pltpu.VMEM((TM, q_dim), BF16),
            pltpu.VMEM((ATTN_KV_HEADS, 2, BLOCK + TM, LANES), BF16),
            pltpu.VMEM((ATTN_KV_HEADS, 2, BLOCK + TM, LANES), BF16),
            pltpu.VMEM((TM, q_dim), BF16),
            pltpu.VMEM((2, ATTN_HEADS, BLOCK, 2 * BLOCK), F32),
            pltpu.VMEM(w_qkv.shape[1:], BF16),
            pltpu.VMEM(w_o.shape[1:], BF16),
        ],
        compiler_params=_params(),
        name="swa_mixer",
    )(sinks, x, g, w_qkv, b_qkv, w_o, b_o)


def _conformer_kernel(x_ref, g_ref, w1_f32, b1_ref, wdw_ref, bdw_ref, lng_ref, lnb_ref,
                      w2_f32, b2_ref, o_ref, ext_s, u_s, w1_ref, w2_ref, *, tiles_per_seq):
    tm = x_ref.shape[0]
    n_slabs = D_MODEL // LANES
    rows = tm // CONF_STRIDE

    @pl.when(pl.program_id(0) == 0)
    def _():
        _to_bf16(w1_f32, w1_ref)
        _to_bf16(w2_f32, w2_ref)

    first = (pl.program_id(0) % tiles_per_seq) == 0
    x = x_ref[...]
    xg, r = _rms_split(x, g_ref[...])
    for j in range(D_MODEL // CHUNK):
        lo, hi = j * CHUNK, (j + 1) * CHUNK
        a = _dot(xg, w1_ref[:, lo:hi]) * r + b1_ref[:, lo:hi]
        gate = _dot(xg, w1_ref[:, D_MODEL + lo:D_MODEL + hi]) * r + b1_ref[:, D_MODEL + lo:D_MODEL + hi]
        u = a * _sigmoid(gate)
        for i in range(CHUNK // LANES):
            k = j * (CHUNK // LANES) + i
            klo, khi = k * LANES, (k + 1) * LANES
            ext_s[k, 0:CONF_CARRY, :] = jnp.where(first, 0.0, ext_s[k, tm:tm + CONF_CARRY, :])
            ext_s[k, CONF_CARRY:CONF_CARRY + tm, :] = u[:, i * LANES:(i + 1) * LANES]
            for part in range(CONF_PARTS):
                n = rows // CONF_PARTS
                base = part * n * CONF_STRIDE
                acc = [jnp.zeros((n, LANES), F32) + bdw_ref[:, klo:khi] for _ in range(CONF_STRIDE)]
                for d in range(CONF_STRIDE - 1, -CONF_WIDTH, -1):
                    src = ext_s[k, pl.ds(CONF_CARRY + base + d, n, stride=CONF_STRIDE), :]
                    for p in range(CONF_STRIDE):
                        delay = p - d
                        if 0 <= delay < CONF_WIDTH:
                            tap = CONF_WIDTH - 1 - delay
                            acc[p] = acc[p] + wdw_ref[tap:tap + 1, klo:khi] * src
                for p in range(CONF_STRIDE):
                    u_s[k, pl.ds(base + p, n, stride=CONF_STRIDE), :] = acc[p]
    u = jnp.concatenate([u_s[k] for k in range(n_slabs)], axis=1)
    mu = jnp.mean(u, axis=-1, keepdims=True)
    uc = u - mu
    y = uc * lax.rsqrt(jnp.mean(uc * uc, axis=-1, keepdims=True) + EPS)
    y = y * lng_ref[...] + lnb_ref[...]
    y = (y * _sigmoid(y)).astype(BF16)
    o_ref[...] = x + _dot(y, w2_ref[...]) + b2_ref[...]


def _conformer_mixer(x, g, w1, b1, wdw, bdw, lng, lnb, w2, b2, *, layer, j, seq):
    n, d = x.shape
    kern = functools.partial(_conformer_kernel, tiles_per_seq=seq // TM)
    return pl.pallas_call(
        kern,
        out_shape=jax.ShapeDtypeStruct((n, d), F32),
        grid=(n // TM,),
        in_specs=[pl.BlockSpec((TM, d), lambda i: (i, 0))]
        + [_layer(g.shape, layer)]
        + [_layer(a.shape, j) for a in (w1, b1, wdw, bdw, lng, lnb, w2, b2)],
        out_specs=pl.BlockSpec((TM, d), lambda i: (i, 0)),
        scratch_shapes=[pltpu.VMEM((d // LANES, CONF_CARRY + TM, LANES), F32),
                        pltpu.VMEM((d // LANES, TM, LANES), F32),
                        pltpu.VMEM(w1.shape[1:], BF16),
                        pltpu.VMEM(w2.shape[1:], BF16)],
        compiler_params=_params(),
        name="conformer_mixer",
    )(x, g, w1, b1, wdw, bdw, lng, lnb, w2, b2)


def _memfold_kernel(mem_ref, g_ref, wkv_ref, wq_ref, wo_ref, m_ref, vw_ref):
    bsz, n_mem, d = mem_ref.shape
    hd = XATTN_HEAD_DIM
    mn = _rms(mem_ref[...].reshape(bsz * n_mem, d), g_ref[...]).astype(BF16)
    k = _dot(mn, wkv_ref[:, :d].astype(BF16)).astype(BF16)
    v = _dot(mn, wkv_ref[:, d:].astype(BF16)).astype(BF16)
    scale = hd ** -0.5
    for h in range(XATTN_HEADS):
        lo, hi = h * hd, (h + 1) * hd
        wq = wq_ref[:, lo:hi].astype(BF16)
        wo = wo_ref[lo:hi, :].astype(BF16)
        for b in range(bsz):
            r0, r1 = b * n_mem, (b + 1) * n_mem
            m_ref[b, :, lo:hi] = (_dot_nt(wq, k[r0:r1, lo:hi]) * scale).astype(BF16)
            vw_ref[b, lo:hi, :] = _dot(v[r0:r1, lo:hi], wo).astype(BF16)


def _memory_fold(mem, g_mem, w_kv, w_q, w_o):
    bsz, m, d = mem.shape
    depth = w_kv.shape[0]
    hm = XATTN_HEADS * m
    return pl.pallas_call(
        _memfold_kernel,
        out_shape=(jax.ShapeDtypeStruct((depth, bsz, d, hm), BF16),
                   jax.ShapeDtypeStruct((depth, bsz, hm, d), BF16)),
        grid=(depth,),
        in_specs=[
            _resident(mem.shape),
            pl.BlockSpec((None, 1, d), lambda l: (l, 0, 0)),
            pl.BlockSpec((None, d, 2 * d), lambda l: (l, 0, 0)),
            pl.BlockSpec((None, d, d), lambda l: (l, 0, 0)),
            pl.BlockSpec((None, d, d), lambda l: (l, 0, 0)),
        ],
        out_specs=(pl.BlockSpec((None, bsz, d, hm), lambda l: (l, 0, 0, 0)),
                   pl.BlockSpec((None, bsz, hm, d), lambda l: (l, 0, 0, 0))),
        compiler_params=_params(),
        name="memory_fold",
    )(mem, g_mem, w_kv, w_q, w_o)


def _xattn_tile(x, g_ref, m_ref, vw_ref, p_s):
    n_mem = m_ref.shape[-1] // XATTN_HEADS
    xg, r = _rms_split(x, g_ref[...])
    scores = [_dot(xg, m_ref[:, h * n_mem:(h + 1) * n_mem]) for h in range(XATTN_HEADS)]
    for h, s in enumerate(scores):
        s = s * r
        e = jnp.exp(s - jnp.max(s, axis=-1, keepdims=True))
        inv = 1.0 / jnp.sum(e, axis=-1, keepdims=True)
        p_s[:, h * n_mem:(h + 1) * n_mem] = (e * inv).astype(BF16)
    return x + _dot(p_s[...], vw_ref[...])


FFN_PREP = D_FF // CHUNK
UP_SLAB = 2 * D_FF // FFN_PREP


def _up_cols(ref, col):
    slab, off = divmod(col, UP_SLAB)
    return ref[slab, :, off:off + CHUNK]


def _xattn_ffn_kernel(x_ref, gx_ref, m_ref, vw_ref, g_ref, wup_slab, cw_ref, wd_slab, gf_ref,
                      o_ref, p_s, act_s, carry_s, wup_s, wd_s, *, tiles_per_seq, final_norm):
    tm = x_ref.shape[0]
    step = pl.program_id(0)

    @pl.when(step < FFN_PREP)
    def _():
        wup_s[step] = wup_slab[...].astype(BF16)
        row = pl.multiple_of(step * CHUNK, CHUNK)
        wd_s[pl.ds(row, CHUNK), :] = wd_slab[...].astype(BF16)

    @pl.when(step >= FFN_PREP)
    def _():
        first = ((step - FFN_PREP) % tiles_per_seq) == 0
        x = _xattn_tile(x_ref[...], gx_ref, m_ref, vw_ref, p_s)
        xg, r = _rms_split(x, g_ref[...])
        for j in range(D_FF // CHUNK):
            lo, hi = j * CHUNK, (j + 1) * CHUNK
            gate = _dot(xg, _up_cols(wup_s, lo)) * r
            up = _dot(xg, _up_cols(wup_s, D_FF + lo)) * r
            prev = jnp.where(first, 0.0, carry_s[:, lo:hi])
            carry_s[:, lo:hi] = gate[tm - SUBLANES:, :]
            cv = (cw_ref[2:3, lo:hi] * gate
                  + cw_ref[1:2, lo:hi] * _shifted(prev, gate, 1)
                  + cw_ref[0:1, lo:hi] * _shifted(prev, gate, 2))
            act_s[:, lo:hi] = (cv * _sigmoid(cv) * up).astype(BF16)
        y = x + _dot(act_s[...], wd_s[...])
        if final_norm:
            y = _rms(y, gf_ref[...])
        o_ref[...] = y


def _xattn_mlp(x, gx, m_all, vw_all, g, w_up, cw, wd, g_final, *, layer, seq, final_norm):
    n, d = x.shape
    hm = m_all.shape[-1]
    tiles_per_seq = seq // TM
    assert UP_SLAB % CHUNK == 0 and D_FF % UP_SLAB == CHUNK % UP_SLAB
    kern = functools.partial(_xattn_ffn_kernel, tiles_per_seq=tiles_per_seq, final_norm=final_norm)
    last = FFN_PREP - 1
    tile = lambda i: (jnp.maximum(i - FFN_PREP, 0), 0)
    batch = lambda i: (layer, jnp.maximum(i - FFN_PREP, 0) // tiles_per_seq, 0, 0)
    return pl.pallas_call(
        kern,
        out_shape=jax.ShapeDtypeStruct((n, d), F32),
        grid=(FFN_PREP + n // TM,),
        in_specs=[
            pl.BlockSpec((TM, d), tile),
            _layer(gx.shape, layer),
            pl.BlockSpec((None, None, d, hm), batch),
            pl.BlockSpec((None, None, hm, d), batch),
            _layer(g.shape, layer),
            pl.BlockSpec((None, d, UP_SLAB), lambda i: (layer, 0, jnp.minimum(i, last))),
            _layer(cw.shape, layer),
            pl.BlockSpec((None, CHUNK, d), lambda i: (layer, jnp.minimum(i, last), 0)),
            _resident(g_final.shape),
        ],
        out_specs=pl.BlockSpec((TM, d), tile),
        scratch_shapes=[pltpu.VMEM((TM, hm), BF16),
                        pltpu.VMEM((TM, D_FF), BF16),
                        pltpu.VMEM((SUBLANES, D_FF), F32),
                        pltpu.VMEM((FFN_PREP, d, UP_SLAB), BF16),
                        pltpu.VMEM((D_FF, d), BF16)],
        compiler_params=_params(),
        name="xattn_mlp",
    )(x, gx, m_all, vw_all, g, w_up, cw, wd, g_final)


def kernel(x, mem, norm_mix, norm_xattn, norm_mem, norm_ffn, norm_final, a_w_in, a_conv, a_w_out, b_w_qkv, b_b_qkv, b_sinks, b_w_o, b_b_o, c_w_pw1, c_b_pw1, c_w_dw, c_b_dw, c_ln_g, c_ln_b, c_w_pw2, c_b_pw2, x_w_q, x_w_kv, x_w_o, f_w_up, f_conv, f_w_down):
    bsz, seq, d = x.shape
    depth = norm_mix.shape[0]
    assert d == D_MODEL and seq % TM == 0 and TM % BLOCK == 0
    row = lambda p: p.reshape(p.shape[0], 1, p.shape[1])

    h = x.reshape(bsz * seq, d)
    m_all, vw_all = _memory_fold(mem, row(norm_mem), x_w_kv, x_w_q, x_w_o)
    g_mix, g_xattn, g_ffn = row(norm_mix), row(norm_xattn), row(norm_ffn)

    for i in range(depth):
        kind, j = i % 3, i // 3
        if kind == 0:
            h = _sconv_mixer(h, g_mix, a_w_in, a_conv, a_w_out, layer=i, j=j, seq=seq)
        elif kind == 1:
            h = _swa_mixer(h, g_mix, b_w_qkv, row(b_b_qkv), b_sinks[j], b_w_o, row(b_b_o),
                           layer=i, j=j, seq=seq)
        else:
            h = _conformer_mixer(h, g_mix, c_w_pw1, row(c_b_pw1), c_w_dw, row(c_b_dw),
                                 row(c_ln_g), row(c_ln_b), c_w_pw2, row(c_b_pw2),
                                 layer=i, j=j, seq=seq)
        h = _xattn_mlp(h, g_xattn, m_all, vw_all, g_ffn, f_w_up, f_conv, f_w_down,
                       norm_final.reshape(1, d), layer=i, seq=seq, final_norm=(i == depth - 1))
    return h.reshape(bsz, seq, d)
```
